```python
import math
import functools
import jax
import jax.numpy as jnp
from jax import lax
import numpy as np

D_MODEL = 2048
BATCH = 16
SEQ = 256
DEPTH = 4
DEC_BATCH = 2
DEC_SEQ = 4096
PAST_LEN = 256

GRID_W = 64
HEAD_DIM = 128
MIX_W = D_MODEL
NA_W = MIX_W // 4
NA_HEADS = NA_W // HEAD_DIM
NA_WIN_R = 8
NA_WIN_C = 16
GQ_W = MIX_W // 2
GQA_Q_HEADS = GQ_W // HEAD_DIM
GQA_KV_HEADS = 2
GQA_GROUP = GQA_Q_HEADS // GQA_KV_HEADS
GKV_W = GQA_KV_HEADS * HEAD_DIM
ROPE_THETA = 10000.0
Q_BLOCK = 128
HY_W = MIX_W - NA_W - GQ_W
HY_EMB = 33
HY_HID = 64
HY_FAST_DECAY = 0.3
HY_SLOW_DECAY = 1.5
HY_TARGET = 0.01
IN_W = 3 * NA_W + GQ_W + 2 * GKV_W + 3 * HY_W
N_GROUPS = 4
EXP_PER_GROUP = 4
N_EXPERTS = N_GROUPS * EXP_PER_GROUP
TOP_K = 2
D_EXPERT = 512
EPS = 1e-6
NEG_INF = -1e30

kernel_name = 'hybrid_nat_gqa_hyena_hmoe_diffusion_step'


def rmsnorm(x, g):
    xf = x.astype(jnp.float32)
    xf = xf * lax.rsqrt(jnp.mean(xf * xf, axis=-1, keepdims=True) + EPS)
    return xf.astype(x.dtype) * g


def modulation(cond, w, b):
    m = jax.nn.silu(cond) @ w + b
    return [p[:, None, :] for p in jnp.split(m, 6, axis=-1)]


def split_in(z):
    sizes = (NA_W, NA_W, NA_W, GQ_W, GKV_W, GKV_W)
    cuts = [int(s) for s in np.cumsum(sizes)]
    return jnp.split(z, cuts, axis=-1)


def blocked_attention(q, k, v):
    b, lq, hk, g, dh = q.shape
    nb = lq // Q_BLOCK
    qb = jnp.moveaxis(q.reshape(b, nb, Q_BLOCK, hk, g, dh), 1, 0)
    scale = dh ** -0.5

    def one_block(qi):
        s = jnp.einsum('bqhgd,bkhd->bhgqk', qi, k).astype(jnp.float32) * scale
        p = jax.nn.softmax(s, axis=-1).astype(v.dtype)
        return jnp.einsum('bhgqk,bkhd->bqhgd', p, v)

    o = lax.map(one_block, qb)
    return jnp.moveaxis(o, 0, 1).reshape(b, lq, hk, g, dh)


def axial_rope(n, dtype):
    t = jnp.arange(n)
    quarter = HEAD_DIM // 2
    inv = 1.0 / (ROPE_THETA ** (jnp.arange(0, quarter, 2, dtype=jnp.float32) / quarter))
    ar = (t // GRID_W).astype(jnp.float32)[:, None] * inv
    ac = (t % GRID_W).astype(jnp.float32)[:, None] * inv
    ang = jnp.concatenate([ar, ar, ac, ac], axis=-1)
    return jnp.cos(ang).astype(dtype), jnp.sin(ang).astype(dtype)


def apply_axial_rope(x, cos, sin):
    seg = x.reshape(x.shape[:-1] + (2, 2, HEAD_DIM // 4))
    rot = jnp.stack([-seg[..., 1, :], seg[..., 0, :]], axis=-2).reshape(x.shape)
    return x * cos[:, None, :] + rot * sin[:, None, :]


def neighborhood_attention(q, k, v, k_ctx, v_ctx, rpb):
    b, n, h, dh = q.shape
    rows = n // GRID_W
    wr = min(NA_WIN_R, rows)
    r = jnp.arange(rows)
    rs = jnp.clip(r - NA_WIN_R // 2, 0, rows - wr)
    row_idx = rs[:, None] + jnp.arange(wr)[None, :]
    cq = jnp.arange(GRID_W)
    cs = jnp.clip(cq - NA_WIN_C // 2, 0, GRID_W - NA_WIN_C)
    col_ok = (cq[None, :] >= cs[:, None]) & (cq[None, :] < cs[:, None] + NA_WIN_C)
    ri = row_idx - r[:, None] + NA_WIN_R - 1
    ci = jnp.clip(cq[None, :] - cq[:, None] + NA_WIN_C - 1, 0, 2 * NA_WIN_C - 2)
    hh = jnp.arange(h)[None, :, None, None, None]
    bias = rpb.astype(jnp.float32)[hh, ri[:, None, None, :, None], ci[None, None, :, None, :]]
    bias = jnp.where(col_ok[None, None, :, None, :], bias, NEG_INF)
    qg = q.reshape(b, rows, GRID_W, h, dh)
    kb = k.reshape(b, rows, GRID_W, h, dh)[:, row_idx]
    vb = v.reshape(b, rows, GRID_W, h, dh)[:, row_idx]
    scale = dh ** -0.5
    s_nb = jnp.einsum('brqhd,brikhd->brhqik', qg, kb).astype(jnp.float32) * scale + bias
    s_nb = s_nb.reshape(b, rows, h, GRID_W, wr * GRID_W)
    s_cx = jnp.einsum('brqhd,bchd->brhqc', qg, k_ctx).astype(jnp.float32) * scale
    p = jax.nn.softmax(jnp.concatenate([s_nb, s_cx], axis=-1), axis=-1).astype(v.dtype)
    p_nb = p[..., :wr * GRID_W].reshape(b, rows, h, GRID_W, wr, GRID_W)
    p_cx = p[..., wr * GRID_W:]
    o = jnp.einsum('brhqik,brikhd->brqhd', p_nb, vb) + jnp.einsum('brhqc,bchd->brqhd', p_cx, v_ctx)
    return o.reshape(b, n, h * dh)


def hyena_filters(L, w1, b1, w2, b2, w3, b3, freq, w_out):
    f32 = jnp.float32
    t01 = jnp.linspace(0.0, 1.0, L, dtype=f32)[:, None]
    bands = (HY_EMB - 1) // 2
    w = (2.0 * math.pi / L) * jnp.arange(L, dtype=f32)[:, None]
    fr = jnp.linspace(1e-4, bands - 1, bands, dtype=f32)[None, :]
    z = jnp.concatenate([t01, jnp.cos(fr * w), -jnp.sin(fr * w)], axis=-1)
    fq = freq.astype(f32)
    hdn = jnp.sin(fq * (z @ w1.astype(f32) + b1.astype(f32)))
    hdn = jnp.sin(fq * (hdn @ w2.astype(f32) + b2.astype(f32)))
    hdn = jnp.sin(fq * (hdn @ w3.astype(f32) + b3.astype(f32)))
    filt = hdn @ w_out.astype(f32)
    max_decay = math.log(HY_TARGET) / HY_FAST_DECAY
    min_decay = math.log(HY_TARGET) / HY_SLOW_DECAY
    deltas = jnp.abs(jnp.linspace(min_decay, max_decay, HY_W, dtype=f32))
    decay = jnp.exp(-t01 * deltas)
    return filt[:, :HY_W] * decay, filt[:, HY_W:] * decay


def hyena_mixer(u, conv_w, conv_b, w1, b1, w2, b2, w3, b3, freq, w_out, bias_d):
    L = u.shape[1]
    up = jnp.pad(u, ((0, 0), (1, 1), (0, 0)))
    uc = up[:, :-2] * conv_w[0] + up[:, 1:-1] * conv_w[1] + up[:, 2:] * conv_w[2] + conv_b
    x0, x1, v = jnp.split(uc, 3, axis=-1)
    zz = (v * x1).astype(jnp.float32)
    h_fwd, h_bwd = hyena_filters(L, w1, b1, w2, b2, w3, b3, freq, w_out)
    ker = jnp.concatenate([h_fwd, jnp.zeros((1, HY_W), jnp.float32), h_bwd[1:][::-1]], axis=0)
    y = jnp.fft.irfft(jnp.fft.rfft(zz, n=2 * L, axis=1) * jnp.fft.rfft(ker, axis=0)[None], n=2 * L, axis=1)[:, :L]
    y = y + zz * bias_d.astype(jnp.float32)
    return x0 * y.astype(u.dtype)


def hier_moe(h, wg, bg, we, be, w_gate, w_up, w_down):
    b, l, d = h.shape
    t = h.reshape(-1, d)
    n = t.shape[0]
    g_logits = (t @ wg).astype(jnp.float32) + bg.astype(jnp.float32)
    g_prob = jax.nn.softmax(g_logits, axis=-1)
    g_sel = jnp.argmax(g_logits, axis=-1)
    e_logits = ((t @ we).astype(jnp.float32) + be.astype(jnp.float32)).reshape(n, N_GROUPS, EXP_PER_GROUP)
    e_in = e_logits[jnp.arange(n), g_sel]
    top_p, top_i = lax.top_k(jax.nn.softmax(e_in, axis=-1), TOP_K)
    top_p = top_p / jnp.sum(top_p, axis=-1, keepdims=True)
    w_top = top_p * jnp.take_along_axis(g_prob, g_sel[:, None], axis=1)
    expert_id = g_sel[:, None] * EXP_PER_GROUP + top_i
    gates = jnp.einsum('nk,nke->ne', w_top, jax.nn.one_hot(expert_id, N_EXPERTS, dtype=jnp.float32))
    a = jnp.einsum('nd,edf->nef', t, w_gate)
    u = jnp.einsum('nd,edf->nef', t, w_up)
    hid = jax.nn.silu(a) * u * gates[:, :, None].astype(t.dtype)
    return jnp.einsum('nef,efd->nd', hid, w_down).reshape(b, l, d)


def context_mixer(z, q_norm_g, k_norm_g, hyp):
    b, L, _ = z.shape
    aq, ak, av, bq, bk, bv, hy = split_in(z)
    ak = ak.reshape(b, L, NA_HEADS, HEAD_DIM)
    av = av.reshape(b, L, NA_HEADS, HEAD_DIM)
    a_out = blocked_attention(aq.reshape(b, L, NA_HEADS, 1, HEAD_DIM), ak, av).reshape(b, L, NA_W)
    bq = rmsnorm(bq.reshape(b, L, GQA_Q_HEADS, HEAD_DIM), q_norm_g)
    bk = rmsnorm(bk.reshape(b, L, GQA_KV_HEADS, HEAD_DIM), k_norm_g)
    bv = bv.reshape(b, L, GQA_KV_HEADS, HEAD_DIM)
    b_out = blocked_attention(bq.reshape(b, L, GQA_KV_HEADS, GQA_GROUP, HEAD_DIM), bk, bv).reshape(b, L, GQ_W)
    c_out = hyena_mixer(hy, *hyp)
    return jnp.concatenate([a_out, b_out, c_out], axis=-1), (ak, av, bk, bv)


def latent_mixer(z, ctx_na_k, ctx_na_v, ctx_gqa_k, ctx_gqa_v, rpb, q_norm_g, k_norm_g, hyp):
    b, n, _ = z.shape
    aq, ak, av, bq, bk, bv, hy = split_in(z)
    a_out = neighborhood_attention(aq.reshape(b, n, NA_HEADS, HEAD_DIM), ak.reshape(b, n, NA_HEADS, HEAD_DIM),
                                   av.reshape(b, n, NA_HEADS, HEAD_DIM), ctx_na_k, ctx_na_v, rpb)
    bq = rmsnorm(bq.reshape(b, n, GQA_Q_HEADS, HEAD_DIM), q_norm_g)
    bk = rmsnorm(bk.reshape(b, n, GQA_KV_HEADS, HEAD_DIM), k_norm_g)
    cos, sin = axial_rope(n, bq.dtype)
    bq = apply_axial_rope(bq, cos, sin)
    bk = apply_axial_rope(bk, cos, sin)
    k_all = jnp.concatenate([bk, ctx_gqa_k], axis=1)
    v_all = jnp.concatenate([bv.reshape(b, n, GQA_KV_HEADS, HEAD_DIM), ctx_gqa_v], axis=1)
    b_out = blocked_attention(bq.reshape(b, n, GQA_KV_HEADS, GQA_GROUP, HEAD_DIM), k_all, v_all).reshape(b, n, GQ_W)
    c_out = hyena_mixer(hy, *hyp)
    return jnp.concatenate([a_out, b_out, c_out], axis=-1), None


def trunk_layer(x, mods, n1, n2, w_in, w_out, mixer, moe_args):
    sh1, sc1, g1, sh2, sc2, g2 = mods
    h = rmsnorm(x, n1) * (1 + sc1) + sh1
    mix, extra = mixer(h @ w_in)
    x = x + g1 * (mix @ w_out)
    h = rmsnorm(x, n2) * (1 + sc2) + sh2
    x = x + g2 * hier_moe(h, *moe_args)
    return x, extra


def setup_inputs(seed: int = 0) -> dict:
    key = jax.random.key(seed)
    ks = iter(list(jax.random.split(key, 48)))
    D = D_MODEL
    L = DEPTH

    def nrm(shape, scale):
        return jax.random.normal(next(ks), shape, jnp.float32) * scale

    return {
        'x_prompt': nrm((BATCH, SEQ, D), 1.0),
        'x_sample': nrm((DEC_BATCH, DEC_SEQ, D), 1.0),
        'c': nrm((DEC_BATCH, D), 1.0),
        'cache_na_k': nrm((DEC_BATCH, DEPTH, PAST_LEN, NA_HEADS, HEAD_DIM), 1.0),
        'cache_na_v': nrm((DEC_BATCH, DEPTH, PAST_LEN, NA_HEADS, HEAD_DIM), 1.0),
        'cache_gqa_k': nrm((DEC_BATCH, DEPTH, PAST_LEN, GQA_KV_HEADS, HEAD_DIM), 1.0),
        'cache_gqa_v': nrm((DEC_BATCH, DEPTH, PAST_LEN, GQA_KV_HEADS, HEAD_DIM), 1.0),
        'c_ctx': nrm((D,), 1.0),
        'norm1_g': 1.0 + nrm((L, D), 0.02),
        'norm2_g': 1.0 + nrm((L, D), 0.02),
        'final_norm_g': 1.0 + nrm((D,), 0.02),
        'ada_w': nrm((L, D, 6 * D), 0.5 * D ** -0.5),
        'ada_b': nrm((L, 6 * D), 0.02),
        'w_in': nrm((L, D, IN_W), D ** -0.5),
        'w_out': nrm((L, MIX_W, D), MIX_W ** -0.5),
        'na_rpb': nrm((L, NA_HEADS, 2 * NA_WIN_R - 1, 2 * NA_WIN_C - 1), 0.1),
        'q_norm_g': 1.0 + nrm((L, HEAD_DIM), 0.02),
        'k_norm_g': 1.0 + nrm((L, HEAD_DIM), 0.02),
        'hy_conv_w': nrm((L, 3, 3 * HY_W), 3 ** -0.5),
        'hy_conv_b': nrm((L, 3 * HY_W), 0.02),
        'hy_w1': nrm((L, HY_EMB, HY_HID), HY_EMB ** -0.5),
        'hy_b1': nrm((L, HY_HID), 0.1),
        'hy_w2': nrm((L, HY_HID, HY_HID), HY_HID ** -0.5),
        'hy_b2': nrm((L, HY_HID), 0.1),
        'hy_w3': nrm((L, HY_HID, HY_HID), HY_HID ** -0.5),
        'hy_b3': nrm((L, HY_HID), 0.1),
        'hy_freq': 1.0 + nrm((L, HY_HID), 0.1),
        'hy_w_out': nrm((L, HY_HID, 2 * HY_W), 0.1 * HY_HID ** -0.5),
        'hy_bias': nrm((L, HY_W), 0.1),
        'router_group_w': nrm((L, D, N_GROUPS), D ** -0.5),
        'router_group_b': nrm((L, N_GROUPS), 0.01),
        'router_expert_w': nrm((L, D, N_EXPERTS), D ** -0.5),
        'router_expert_b': nrm((L, N_EXPERTS), 0.01),
        'moe_w_gate': nrm((L, N_EXPERTS, D, D_EXPERT), D ** -0.5),
        'moe_w_up': nrm((L, N_EXPERTS, D, D_EXPERT), D ** -0.5),
        'moe_w_down': nrm((L, N_EXPERTS, D_EXPERT, D), D_EXPERT ** -0.5),
    }


def reference(x_prompt, x_sample, c, cache_na_k, cache_na_v, cache_gqa_k, cache_gqa_v, c_ctx,
              norm1_g, norm2_g, final_norm_g, ada_w, ada_b, w_in, w_out, na_rpb, q_norm_g, k_norm_g,
              hy_conv_w, hy_conv_b, hy_w1, hy_b1, hy_w2, hy_b2, hy_w3, hy_b3, hy_freq, hy_w_out, hy_bias,
              router_group_w, router_group_b, router_expert_w, router_expert_b,
              moe_w_gate, moe_w_up, moe_w_down):
    xp = x_prompt
    xs = x_sample
    na_k, na_v, gq_k, gq_v = [], [], [], []
    for l in range(DEPTH):
        hyp = (hy_conv_w[l], hy_conv_b[l], hy_w1[l], hy_b1[l], hy_w2[l], hy_b2[l], hy_w3[l], hy_b3[l],
               hy_freq[l], hy_w_out[l], hy_bias[l])
        moe_args = (router_group_w[l], router_group_b[l], router_expert_w[l], router_expert_b[l],
                    moe_w_gate[l], moe_w_up[l], moe_w_down[l])
        mods_ctx = modulation(c_ctx[None, :], ada_w[l], ada_b[l])
        ctx_fn = functools.partial(context_mixer, q_norm_g=q_norm_g[l], k_norm_g=k_norm_g[l], hyp=hyp)
        xp, (ak, av, bk, bv) = trunk_layer(xp, mods_ctx, norm1_g[l], norm2_g[l], w_in[l], w_out[l], ctx_fn, moe_args)
        na_k.append(ak)
        na_v.append(av)
        gq_k.append(bk)
        gq_v.append(bv)
        mods_lat = modulation(c, ada_w[l], ada_b[l])
        lat_fn = functools.partial(latent_mixer, ctx_na_k=cache_na_k[:, l], ctx_na_v=cache_na_v[:, l],
                                   ctx_gqa_k=cache_gqa_k[:, l], ctx_gqa_v=cache_gqa_v[:, l], rpb=na_rpb[l],
                                   q_norm_g=q_norm_g[l], k_norm_g=k_norm_g[l], hyp=hyp)
        xs, _ = trunk_layer(xs, mods_lat, norm1_g[l], norm2_g[l], w_in[l], w_out[l], lat_fn, moe_args)
    y_prompt = rmsnorm(xp, final_norm_g)
    y_sample = rmsnorm(xs, final_norm_g)
    state_na_k = jnp.stack(na_k, axis=1)
    state_na_v = jnp.stack(na_v, axis=1)
    state_gqa_k = jnp.stack(gq_k, axis=1)
    state_gqa_v = jnp.stack(gq_v, axis=1)
    return (y_prompt, y_sample, state_na_k, state_na_v, state_gqa_k, state_gqa_v)
```

```python
import functools
import math

import numpy as np
import jax
import jax.numpy as jnp
from jax import lax
from jax.experimental import pallas as pl
from jax.experimental.pallas import tpu as pltpu

F32 = jnp.float32
BF16 = jnp.bfloat16
HIGHEST = lax.Precision.HIGHEST

GRID_W = 64
HEAD_DIM = 128
NA_WIN_R = 8
NA_WIN_C = 16
GQA_KV_HEADS = 2
ROPE_THETA = 10000.0
HY_EMB = 33
HY_HID = 64
HY_FAST_DECAY = 0.3
HY_SLOW_DECAY = 1.5
HY_TARGET = 0.01
N_GROUPS = 4
EXP_PER_GROUP = 4
N_EXPERTS = N_GROUPS * EXP_PER_GROUP
TOP_K = 2
EPS = 1e-6
NEG_INF = -1e30

LANES = 128
VMEM_LIMIT = 52 * 1024 * 1024


def _cparams(sem, vmem=VMEM_LIMIT):
    return pltpu.CompilerParams(dimension_semantics=sem, vmem_limit_bytes=vmem)


def _dot(a, b):
    return jnp.dot(a, b, preferred_element_type=F32)


def _dot_t(a, b):
    return lax.dot_general(a, b, (((1,), (1,)), ((), ())), preferred_element_type=F32)


def _dot_hi(a, b):
    return jnp.dot(a, b, preferred_element_type=F32, precision=HIGHEST)


def _mod_kernel(c_ref, w_ref, b_ref, o_ref):
    c = c_ref[...]
    s = c * (1.0 / (1.0 + jnp.exp(-c)))
    o_ref[...] = _dot_hi(s, w_ref[...]) + b_ref[...]


def modulation_all(cond, ada_w, ada_b, tn=1536):
    depth, d, n = ada_w.shape
    r = cond.shape[0]
    return pl.pallas_call(
        _mod_kernel,
        grid=(depth, n // tn),
        in_specs=[pl.BlockSpec((r, d), lambda l, j: (0, 0)),
                  pl.BlockSpec((None, d, tn), lambda l, j: (l, 0, j)),
                  pl.BlockSpec((None, 1, tn), lambda l, j: (l, 0, j))],
        out_specs=pl.BlockSpec((None, r, tn), lambda l, j: (l, 0, j)),
        out_shape=jax.ShapeDtypeStruct((depth, r, n), F32),
        compiler_params=_cparams(("arbitrary", "arbitrary")),
        name="modulation",
    )(cond, ada_w, ada_b.reshape(depth, 1, n))


def _rms(x):
    return x * lax.rsqrt(jnp.mean(x * x, axis=-1, keepdims=True) + EPS)


def _norm_mod_kernel(s_ref, x_ref, g_ref, sc_ref, sh_ref, o_ref):
    h = _rms(x_ref[...]) * g_ref[...]
    o_ref[...] = (h * (1.0 + sc_ref[...]) + sh_ref[...]).astype(o_ref.dtype)


def norm_mod(x, g, mods, sc_idx, sh_idx, seg_ids, tm, out_dtype):
    nt, d = x.shape
    gs = pltpu.PrefetchScalarGridSpec(
        num_scalar_prefetch=1, grid=(nt // tm,),
        in_specs=[pl.BlockSpec((tm, d), lambda i, s: (i, 0)),
                  pl.BlockSpec((1, d), lambda i, s: (0, 0)),
                  pl.BlockSpec((None, 1, d), lambda i, s: (s[i], 0, sc_idx)),
                  pl.BlockSpec((None, 1, d), lambda i, s: (s[i], 0, sh_idx))],
        out_specs=pl.BlockSpec((tm, d), lambda i, s: (i, 0)))
    return pl.pallas_call(
        _norm_mod_kernel, grid_spec=gs,
        out_shape=jax.ShapeDtypeStruct((nt, d), out_dtype),
        compiler_params=_cparams(("arbitrary",)),
        name="norm_mod",
    )(seg_ids, x, g.reshape(1, d), mods, mods)


def _combine_kernel(s_ref, x_ref, y_ref, g2_ref, g_ref, sc_ref, sh_ref, x2_ref, h_ref, *, modulate):
    d = x_ref.shape[1]
    x2 = x_ref[...] + g2_ref[...] * (y_ref[:, :d] + y_ref[:, d:])
    x2_ref[...] = x2
    h = _rms(x2) * g_ref[...]
    if modulate:
        h = h * (1.0 + sc_ref[...]) + sh_ref[...]
    h_ref[...] = h.astype(h_ref.dtype)


def combine_norm(x1, y2, g, mods_prev, mods_next, seg_ids, tm, modulate, h_dtype):
    nt, d = x1.shape
    gs = pltpu.PrefetchScalarGridSpec(
        num_scalar_prefetch=1, grid=(nt // tm,),
        in_specs=[pl.BlockSpec((tm, d), lambda i, s: (i, 0)),
                  pl.BlockSpec((tm, 2 * d), lambda i, s: (i, 0)),
                  pl.BlockSpec((None, 1, d), lambda i, s: (s[i], 0, 5)),
                  pl.BlockSpec((1, d), lambda i, s: (0, 0)),
                  pl.BlockSpec((None, 1, d), lambda i, s: (s[i], 0, 1)),
                  pl.BlockSpec((None, 1, d), lambda i, s: (s[i], 0, 0))],
        out_specs=[pl.BlockSpec((tm, d), lambda i, s: (i, 0)),
                   pl.BlockSpec((tm, d), lambda i, s: (i, 0))])
    return pl.pallas_call(
        functools.partial(_combine_kernel, modulate=modulate), grid_spec=gs,
        out_shape=[jax.ShapeDtypeStruct((nt, d), F32), jax.ShapeDtypeStruct((nt, d), h_dtype)],
        compiler_params=_cparams(("arbitrary",)),
        name="combine_norm",
    )(seg_ids, x1, y2, mods_prev, g.reshape(1, d), mods_next, mods_next)


def _mm_kernel(a_ref, w_ref, o_ref, wbf_ref):
    @pl.when(pl.program_id(1) == 0)
    def _():
        wbf_ref[...] = w_ref[...].astype(BF16)

    o_ref[...] = _dot(a_ref[...], wbf_ref[...])


def in_proj(h, w_in, layer, tm=512, tn=768):
    nt, d = h.shape
    n = w_in.shape[2]
    return pl.pallas_call(
        _mm_kernel,
        grid=(n // tn, nt // tm),
        in_specs=[pl.BlockSpec((tm, d), lambda j, i: (i, 0)),
                  pl.BlockSpec((None, d, tn), lambda j, i: (layer, 0, j))],
        out_specs=pl.BlockSpec((tm, tn), lambda j, i: (i, j)),
        out_shape=jax.ShapeDtypeStruct((nt, n), F32),
        scratch_shapes=[pltpu.VMEM((d, tn), BF16)],
        compiler_params=_cparams(("arbitrary", "arbitrary")),
        name="in_proj",
    )(h, w_in)


def _out_proj_kernel(s_ref, ca_ref, cb_ref, cc_ref, la_ref, lb_ref, lc_ref, w_ref, x_ref, g1_ref, o_ref,
                     wbf_ref, *, nci):
    i = pl.program_id(1)

    @pl.when(i == 0)
    def _():
        wbf_ref[...] = w_ref[...].astype(BF16)

    def project(a_ref, b_ref, c_ref):
        wa = a_ref.shape[1]
        wb = b_ref.shape[1]
        acc = _dot(a_ref[...], wbf_ref[0:wa, :])
        acc += _dot(b_ref[...], wbf_ref[wa:wa + wb, :])
        acc += _dot(c_ref[...], wbf_ref[wa + wb:, :])
        o_ref[...] = x_ref[...] + g1_ref[...] * acc

    @pl.when(i < nci)
    def _():
        project(ca_ref, cb_ref, cc_ref)

    @pl.when(i >= nci)
    def _():
        project(la_ref, lb_ref, lc_ref)


def out_proj(mix_ctx, mix_lat, w_out, layer, x, mods, seg_ids, tm=512, tn=512):
    nt, d = x.shape
    nci = mix_ctx[0].shape[0] // tm
    nj = d // tn
    cspec = lambda a: pl.BlockSpec((tm, a.shape[1]), lambda j, i, s: (jnp.minimum(i, nci - 1), 0))
    lspec = lambda a: pl.BlockSpec((tm, a.shape[1]), lambda j, i, s: (jnp.maximum(i - nci, 0), 0))
    gs = pltpu.PrefetchScalarGridSpec(
        num_scalar_prefetch=1, grid=(nj, nt // tm),
        in_specs=[cspec(mix_ctx[0]), cspec(mix_ctx[1]), cspec(mix_ctx[2]),
                  lspec(mix_lat[0]), lspec(mix_lat[1]), lspec(mix_lat[2]),
                  pl.BlockSpec((None, d, tn), lambda j, i, s: (layer, 0, j)),
                  pl.BlockSpec((tm, tn), lambda j, i, s: (i, j)),
                  pl.BlockSpec((None, 1, tn), lambda j, i, s: (s[i], 0, 2 * nj + j))],
        out_specs=pl.BlockSpec((tm, tn), lambda j, i, s: (i, j)),
        scratch_shapes=[pltpu.VMEM((d, tn), BF16)])
    return pl.pallas_call(
        functools.partial(_out_proj_kernel, nci=nci), grid_spec=gs,
        out_shape=jax.ShapeDtypeStruct((nt, d), F32),
        compiler_params=_cparams(("arbitrary", "arbitrary")),
        name="out_proj",
    )(seg_ids, *mix_ctx, *mix_lat, w_out, x, mods)


def _softmax_pv(s, v):
    m = jnp.max(s, axis=-1, keepdims=True)
    p = jnp.exp(s - m)
    l = jnp.sum(p, axis=-1, keepdims=True)
    return _dot(p.astype(BF16), v) / l


def _ctx_attn_kernel(za_ref, zb_ref, qg_ref, kg_ref, a_ref, b_ref, bk_ref, *, na_heads, gq_heads):
    dh = HEAD_DIM
    scale = dh ** -0.5
    L = za_ref.shape[0]
    na_w = na_heads * dh
    for h in range(na_heads):
        q = za_ref[:, h * dh:(h + 1) * dh].astype(BF16)
        k = za_ref[:, na_w + h * dh:na_w + (h + 1) * dh].astype(BF16)
        v = za_ref[:, 2 * na_w + h * dh:2 * na_w + (h + 1) * dh].astype(BF16)
        s = _dot_t(q, k) * scale
        a_ref[:, h * dh:(h + 1) * dh] = _softmax_pv(s, v).astype(a_ref.dtype)
    gq_w = gq_heads * dh
    group = gq_heads // GQA_KV_HEADS
    for kv in range(GQA_KV_HEADS):
        kn = _rms(zb_ref[:, gq_w + kv * dh:gq_w + (kv + 1) * dh]) * kg_ref[...]
        bk_ref[:, kv * dh:(kv + 1) * dh] = kn
        knb = kn.astype(BF16)
        v = zb_ref[:, gq_w + (GQA_KV_HEADS + kv) * dh:gq_w + (GQA_KV_HEADS + kv + 1) * dh].astype(BF16)
        for g in range(group):
            h = kv * group + g
            qn = (_rms(zb_ref[:, h * dh:(h + 1) * dh]) * qg_ref[...]).astype(BF16)
            s = _dot_t(qn, knb) * scale
            b_ref[:, h * dh:(h + 1) * dh] = _softmax_pv(s, v).astype(b_ref.dtype)


def ctx_attn(z, q_g, k_g, bc, lc, na_w, gq_w):
    dh = HEAD_DIM
    wa = 3 * na_w
    wb = gq_w + 2 * GQA_KV_HEADS * dh
    assert wa == wb
    kern = functools.partial(_ctx_attn_kernel, na_heads=na_w // dh, gq_heads=gq_w // dh)
    return pl.pallas_call(
        kern, grid=(bc,),
        in_specs=[pl.BlockSpec((lc, wa), lambda b: (b, 0)),
                  pl.BlockSpec((lc, wb), lambda b: (b, 1)),
                  pl.BlockSpec((1, dh), lambda b: (0, 0)),
                  pl.BlockSpec((1, dh), lambda b: (0, 0))],
        out_specs=[pl.BlockSpec((lc, na_w), lambda b: (b, 0)),
                   pl.BlockSpec((lc, gq_w), lambda b: (b, 0)),
                   pl.BlockSpec((lc, GQA_KV_HEADS * dh), lambda b: (b, 0))],
        out_shape=[jax.ShapeDtypeStruct((bc * lc, na_w), BF16),
                   jax.ShapeDtypeStruct((bc * lc, gq_w), BF16),
                   jax.ShapeDtypeStruct((bc * lc, GQA_KV_HEADS * dh), F32)],
        compiler_params=_cparams(("arbitrary",)),
        name="ctx_attn",
    )(z, z, q_g.reshape(1, dh), k_g.reshape(1, dh))


def _rope_kernel(cos_ref, sa_ref, sb_ref):
    n, dh = cos_ref.shape
    quarter = dh // 2
    t = lax.broadcasted_iota(jnp.int32, (n, dh), 0)
    lane = lax.broadcasted_iota(jnp.int32, (n, dh), 1)
    f = (lane & (quarter // 2 - 1)).astype(F32)
    inv = jnp.exp(-(2.0 * f / quarter) * math.log(ROPE_THETA))
    pos = jnp.where(lane < quarter, t >> int(math.log2(GRID_W)), t & (GRID_W - 1)).astype(F32)
    ang = pos * inv
    sin = jnp.sin(ang)
    first = (lane & (quarter - 1)) < (quarter // 2)
    cos_ref[...] = jnp.cos(ang)
    sa_ref[...] = jnp.where(first, -sin, 0.0)
    sb_ref[...] = jnp.where(first, 0.0, sin)


def rope_tables(n):
    shp = jax.ShapeDtypeStruct((n, HEAD_DIM), F32)
    return pl.pallas_call(_rope_kernel, out_shape=[shp, shp, shp], name="rope_tables")()


def _rope(x, cos, sa, sb):
    q4 = HEAD_DIM // 4
    return x * cos + pltpu.roll(x, HEAD_DIM - q4, 1) * sa + pltpu.roll(x, q4, 1) * sb


def _lat_prep_kernel(z_ref, cos_ref, sa_ref, sb_ref, qg_ref, kg_ref, q_ref, k_ref, v_ref, *, gq_heads):
    dh = HEAD_DIM
    scale = dh ** -0.5
    cos, sa, sb = cos_ref[...], sa_ref[...], sb_ref[...]
    for h in range(gq_heads):
        qn = _rms(z_ref[:, h * dh:(h + 1) * dh]) * qg_ref[...]
        q_ref[h] = (_rope(qn, cos, sa, sb) * scale).astype(BF16)
    gq_w = gq_heads * dh
    for kv in range(GQA_KV_HEADS):
        kn = _rms(z_ref[:, gq_w + kv * dh:gq_w + (kv + 1) * dh]) * kg_ref[...]
        k_ref[kv] = _rope(kn, cos, sa, sb).astype(BF16)
        v_ref[kv] = z_ref[:, gq_w + (GQA_KV_HEADS + kv) * dh:gq_w + (GQA_KV_HEADS + kv + 1) * dh].astype(BF16)


def lat_prep(z, ropes, q_g, k_g, nc, bl, ll, gq_w, tl=512):
    dh = HEAD_DIM
    gq_heads = gq_w // dh
    wb = gq_w + 2 * GQA_KV_HEADS * dh
    nb = ll // tl
    off = nc // tl
    cos, sa, sb = ropes
    tab = pl.BlockSpec((tl, dh), lambda b, i: (i, 0))
    one = pl.BlockSpec((1, dh), lambda b, i: (0, 0))
    return pl.pallas_call(
        functools.partial(_lat_prep_kernel, gq_heads=gq_heads),
        grid=(bl, nb),
        in_specs=[pl.BlockSpec((tl, wb), lambda b, i: (off + b * nb + i, 1)), tab, tab, tab, one, one],
        out_specs=[pl.BlockSpec((None, gq_heads, tl, dh), lambda b, i: (b, 0, i, 0)),
                   pl.BlockSpec((None, GQA_KV_HEADS, tl, dh), lambda b, i: (b, 0, i, 0)),
                   pl.BlockSpec((None, GQA_KV_HEADS, tl, dh), lambda b, i: (b, 0, i, 0))],
        out_shape=[jax.ShapeDtypeStruct((bl, gq_heads, ll, dh), BF16),
                   jax.ShapeDtypeStruct((bl, GQA_KV_HEADS, ll, dh), BF16),
                   jax.ShapeDtypeStruct((bl, GQA_KV_HEADS, ll, dh), BF16)],
        compiler_params=_cparams(("arbitrary", "arbitrary")),
        name="lat_prep",
    )(z, cos, sa, sb, q_g.reshape(1, dh), k_g.reshape(1, dh))


def _lat_gqa_kernel(q_ref, k_ref, v_ref, ck_ref, cv_ref, o_ref, *, tk):
    group, tq, dh = q_ref.shape
    q = q_ref[...].reshape(group * tq, dh)
    ll = k_ref.shape[0]

    def update(carry, kc, vc):
        m, l, acc = carry
        s = _dot_t(q, kc)
        m_new = jnp.maximum(m, jnp.max(s, axis=-1, keepdims=True))
        alpha = jnp.exp(m - m_new)
        p = jnp.exp(s - m_new)
        l = alpha * l + jnp.sum(p, axis=-1, keepdims=True)
        acc = alpha * acc + _dot(p.astype(BF16), vc)
        return m_new, l, acc

    def body(c, carry):
        st = pl.multiple_of(c * tk, tk)
        return update(carry, k_ref[pl.ds(st, tk), :], v_ref[pl.ds(st, tk), :])

    init = (jnp.full((group * tq, 1), -jnp.inf, F32), jnp.zeros((group * tq, 1), F32),
            jnp.zeros((group * tq, dh), F32))
    carry = lax.fori_loop(0, ll // tk, body, init)
    m, l, acc = update(carry, ck_ref[...].astype(BF16), cv_ref[...].astype(BF16))
    out = acc / l
    for g in range(group):
        o_ref[:, g * dh:(g + 1) * dh] = out[g * tq:(g + 1) * tq].astype(o_ref.dtype)


def lat_gqa(q, k, v, cache_k, cache_v, layer, tq=256, tk=512):
    bl, gq_heads, ll, dh = q.shape
    group = gq_heads // GQA_KV_HEADS
    past = cache_k.shape[2]
    nq = ll // tq
    tk = min(tk, ll)
    ck = pl.BlockSpec((None, None, past, dh), lambda b, h, i: (b, layer, 0, h))
    return pl.pallas_call(
        functools.partial(_lat_gqa_kernel, tk=tk),
        grid=(bl, GQA_KV_HEADS, nq),
        in_specs=[pl.BlockSpec((None, group, tq, dh), lambda b, h, i: (b, h, i, 0)),
                  pl.BlockSpec((None, None, ll, dh), lambda b, h, i: (b, h, 0, 0)),
                  pl.BlockSpec((None, None, ll, dh), lambda b, h, i: (b, h, 0, 0)),
                  ck, ck],
        out_specs=pl.BlockSpec((tq, group * dh), lambda b, h, i: (b * nq + i, h)),
        out_shape=jax.ShapeDtypeStruct((bl * ll, gq_heads * dh), BF16),
        compiler_params=_cparams(("arbitrary", "arbitrary", "arbitrary")),
        name="lat_gqa",
    )(q, k, v, cache_k, cache_v)


def _na_bias_kernel(rpb_ref, o_ref):
    h = pl.program_id(0)
    w = GRID_W
    qc = lax.broadcasted_iota(jnp.int32, (w, w), 0)
    kc = lax.broadcasted_iota(jnp.int32, (w, w), 1)
    ci = jnp.clip(kc - qc + NA_WIN_C - 1, 0, 2 * NA_WIN_C - 2)
    cs = jnp.clip(qc - NA_WIN_C // 2, 0, w - NA_WIN_C)
    col_ok = (kc >= cs) & (kc < cs + NA_WIN_C)
    nr, ncol = 2 * NA_WIN_R - 1, 2 * NA_WIN_C - 1
    tiles = []
    for dr in range(nr):
        t = jnp.zeros((w, w), F32)
        for c in range(ncol):
            t = jnp.where(ci == c, rpb_ref[h, dr * ncol + c], t)
        tiles.append(jnp.where(col_ok, t, NEG_INF))
    for d0 in range(NA_WIN_R):
        for j in range(NA_WIN_R):
            o_ref[d0, :, j * w:(j + 1) * w] = tiles[d0 + j]


def na_bias(rpb_l):
    heads = rpb_l.shape[0]
    flat = rpb_l.reshape(heads, -1)
    return pl.pallas_call(
        _na_bias_kernel, grid=(heads,),
        in_specs=[pl.BlockSpec(memory_space=pltpu.SMEM)],
        out_specs=pl.BlockSpec((None, NA_WIN_R, GRID_W, NA_WIN_R * GRID_W), lambda h: (h, 0, 0, 0)),
        out_shape=jax.ShapeDtypeStruct((heads, NA_WIN_R, GRID_W, NA_WIN_R * GRID_W), F32),
        compiler_params=_cparams(("arbitrary",)),
        name="na_bias",
    )(flat)


def _lat_na_kernel(q_ref, k_ref, v_ref, ck_ref, cv_ref, bias_ref, o_ref, kb_ref, vb_ref, *, rb, rows):
    w = GRID_W
    scale = HEAD_DIM ** -0.5
    i = pl.program_id(2)

    @pl.when(i == 0)
    def _():
        kb_ref[...] = k_ref[...].astype(BF16)
        vb_ref[...] = v_ref[...].astype(BF16)

    ck = ck_ref[...].astype(BF16)
    cv = cv_ref[...].astype(BF16)
    for j in range(rb):
        r = i * rb + j
        rs = jnp.clip(r - NA_WIN_R // 2, 0, rows - NA_WIN_R)
        d0 = rs - r + NA_WIN_R - 1
        st = pl.multiple_of(rs * w, w)
        q = q_ref[j * w:(j + 1) * w, :].astype(BF16)
        kw = kb_ref[pl.ds(st, NA_WIN_R * w), :]
        vw = vb_ref[pl.ds(st, NA_WIN_R * w), :]
        s1 = _dot_t(q, kw) * scale + bias_ref[d0]
        s2 = _dot_t(q, ck) * scale
        m = jnp.maximum(jnp.max(s1, axis=-1, keepdims=True), jnp.max(s2, axis=-1, keepdims=True))
        p1 = jnp.exp(s1 - m)
        p2 = jnp.exp(s2 - m)
        l = jnp.sum(p1, axis=-1, keepdims=True) + jnp.sum(p2, axis=-1, keepdims=True)
        o = (_dot(p1.astype(BF16), vw) + _dot(p2.astype(BF16), cv)) / l
        o_ref[j * w:(j + 1) * w, :] = o.astype(o_ref.dtype)


def lat_na(z, cache_k, cache_v, layer, bias, nc, bl, ll, na_w, rb=8):
    dh = HEAD_DIM
    heads = na_w // dh
    rows = ll // GRID_W
    assert rows >= NA_WIN_R and nc % ll == 0 and rows % rb == 0
    past = cache_k.shape[2]
    nb = rows // rb
    tq = rb * GRID_W
    off = nc // tq
    lat0 = nc // ll
    ck = pl.BlockSpec((None, None, past, dh), lambda b, h, i: (b, layer, 0, h))
    return pl.pallas_call(
        functools.partial(_lat_na_kernel, rb=rb, rows=rows),
        grid=(bl, heads, nb),
        in_specs=[pl.BlockSpec((tq, dh), lambda b, h, i: (off + b * nb + i, h)),
                  pl.BlockSpec((ll, dh), lambda b, h, i: (lat0 + b, heads + h)),
                  pl.BlockSpec((ll, dh), lambda b, h, i: (lat0 + b, 2 * heads + h)),
                  ck, ck,
                  pl.BlockSpec((None, NA_WIN_R, GRID_W, NA_WIN_R * GRID_W), lambda b, h, i: (h, 0, 0, 0))],
        out_specs=pl.BlockSpec((tq, dh), lambda b, h, i: (b * nb + i, h)),
        out_shape=jax.ShapeDtypeStruct((bl * ll, na_w), BF16),
        scratch_shapes=[pltpu.VMEM((ll, dh), BF16), pltpu.VMEM((ll, dh), BF16)],
        compiler_params=_cparams(("arbitrary", "arbitrary", "arbitrary")),
        name="lat_na",
    )(z, z, z, cache_k, cache_v, bias)


def _hy_pre_kernel(u0_ref, u1_ref, u2_ref, w0_ref, w1_ref, w2_ref, b0_ref, b1_ref, b2_ref,
                   x0_ref, zz_ref, zb_ref):
    L = u0_ref.shape[0]
    t = lax.broadcasted_iota(jnp.int32, u0_ref.shape, 0)

    def conv(u_ref, w_ref, b_ref):
        u = u_ref[...]
        prev = jnp.where(t == 0, 0.0, pltpu.roll(u, 1, 0))
        nxt = jnp.where(t == L - 1, 0.0, pltpu.roll(u, L - 1, 0))
        return prev * w_ref[0:1, :] + u * w_ref[1:2, :] + nxt * w_ref[2:3, :] + b_ref[...]

    x0_ref[...] = conv(u0_ref, w0_ref, b0_ref)
    zz = conv(u2_ref, w2_ref, b2_ref) * conv(u1_ref, w1_ref, b1_ref)
    zz_ref[...] = zz
    zb_ref[...] = zz.astype(BF16)


def hy_pre(z, conv_w, conv_b, row0, nb, L, col0, hy_w):
    cb = LANES
    nj = hy_w // cb
    c0 = col0 // cb
    r0 = row0 // L
    assert row0 % L == 0
    u = lambda part: pl.BlockSpec((L, cb), lambda b, j: (r0 + b, c0 + part * nj + j))
    wspec = lambda part: pl.BlockSpec((3, cb), lambda b, j: (0, part * nj + j))
    bspec = lambda part: pl.BlockSpec((1, cb), lambda b, j: (0, part * nj + j))
    o = pl.BlockSpec((None, L, cb), lambda b, j: (b, 0, j))
    return pl.pallas_call(
        _hy_pre_kernel, grid=(nb, nj),
        in_specs=[u(0), u(1), u(2), wspec(0), wspec(1), wspec(2), bspec(0), bspec(1), bspec(2)],
        out_specs=[o, o, o],
        out_shape=[jax.ShapeDtypeStruct((nb, L, hy_w), F32), jax.ShapeDtypeStruct((nb, L, hy_w), F32),
                   jax.ShapeDtypeStruct((nb, L, hy_w), BF16)],
        compiler_params=_cparams(("arbitrary", "arbitrary")),
        name="hy_pre",
    )(z, z, z, conv_w, conv_w, conv_w, conv_b.reshape(1, -1), conv_b.reshape(1, -1), conv_b.reshape(1, -1))


def _hy_filter_kernel(w1_ref, b1_ref, w2_ref, b2_ref, w3_ref, b3_ref, fq_ref, wo_ref, o_ref, *, hy_w):
    L = o_ref.shape[1]
    bands = (HY_EMB - 1) // 2
    ti = lax.broadcasted_iota(jnp.int32, (L, LANES), 0).astype(F32)
    lane = lax.broadcasted_iota(jnp.int32, (L, LANES), 1)
    t01 = ti / (L - 1)
    w = (2.0 * math.pi / L) * ti
    band = ((lane - 1) % bands).astype(F32)
    fr = 1e-4 + band * ((bands - 1 - 1e-4) / (bands - 1))
    ang = fr * w
    feat = jnp.where(lane == 0, t01,
                     jnp.where(lane <= bands, jnp.cos(ang),
                               jnp.where(lane <= 2 * bands, -jnp.sin(ang), 0.0)))
    fq = fq_ref[...]
    hdn = jnp.sin(fq * (_dot_hi(feat, w1_ref[...]) + b1_ref[...]))
    hdn = jnp.sin(fq * (_dot_hi(hdn, w2_ref[...]) + b2_ref[...]))
    hdn = jnp.sin(fq * (_dot_hi(hdn, w3_ref[...]) + b3_ref[...]))
    filt = _dot_hi(hdn, wo_ref[...])
    max_decay = math.log(HY_TARGET) / HY_FAST_DECAY
    min_decay = math.log(HY_TARGET) / HY_SLOW_DECAY
    ch = lax.broadcasted_iota(jnp.int32, (L, hy_w), 1).astype(F32)
    deltas = jnp.abs(min_decay + ch * ((max_decay - min_decay) / (hy_w - 1)))
    tt = lax.broadcasted_iota(jnp.int32, (L, hy_w), 0)
    decay = jnp.exp(-(tt.astype(F32) / (L - 1)) * deltas)
    o_ref[0] = (filt[:, :hy_w] * decay).astype(o_ref.dtype)
    o_ref[1] = jnp.where(tt == 0, 0.0, filt[:, hy_w:] * decay).astype(o_ref.dtype)


def _pad2(a, r, c):
    return jnp.pad(a, ((0, r - a.shape[0]), (0, c - a.shape[1])))


def hy_filter(L, w1, b1, w2, b2, w3, b3, freq, w_out):
    hy_w = w_out.shape[1] // 2
    p = LANES
    args = (_pad2(w1, p, p), _pad2(b1[None], 1, p), _pad2(w2, p, p), _pad2(b2[None], 1, p),
            _pad2(w3, p, p), _pad2(b3[None], 1, p), _pad2(freq[None], 1, p), _pad2(w_out, p, 2 * hy_w))
    return pl.pallas_call(
        functools.partial(_hy_filter_kernel, hy_w=hy_w),
        out_shape=jax.ShapeDtypeStruct((2, L, hy_w), BF16),
        compiler_params=_cparams(None),
        name="hy_filter",
    )(*args)


def _dft_tables_kernel(c_ref, s1_ref, s2_ref, *, L):
    tk = c_ref.shape[0]
    n2 = 2 * L
    theta = 2.0 * math.pi / n2
    k = pl.program_id(0) * tk + lax.broadcasted_iota(jnp.int32, (tk, LANES), 0)
    lane = lax.broadcasted_iota(jnp.int32, (tk, LANES), 1)
    a = ((k * lane) & (n2 - 1)).astype(F32) * theta
    b = ((k * LANES * lane) & (n2 - 1)).astype(F32) * theta
    ca, sa, cb, sb = jnp.cos(a), jnp.sin(a), jnp.cos(b), jnp.sin(b)
    sign_k = (1 - 2 * (k & 1)).astype(F32)
    for j in range(L // LANES):
        cbj = cb[:, j:j + 1]
        sbj = sb[:, j:j + 1]
        c = ca * cbj - sa * sbj
        s = sa * cbj + ca * sbj
        t = lane + j * LANES
        sign_t = (1 - 2 * (t & 1)).astype(F32)
        c_ref[:, j * LANES:(j + 1) * LANES] = c.astype(BF16)
        s1_ref[:, j * LANES:(j + 1) * LANES] = jnp.where(k == 0, sign_t, s).astype(BF16)
        s2_ref[:, j * LANES:(j + 1) * LANES] = jnp.where(t == 0, sign_k, s).astype(BF16)


def dft_tables(L, tk=256):
    tk = min(tk, L)
    shp = jax.ShapeDtypeStruct((L, L), BF16)
    spec = pl.BlockSpec((tk, L), lambda i: (i, 0))
    return pl.pallas_call(
        functools.partial(_dft_tables_kernel, L=L), grid=(L // tk,),
        out_specs=[spec, spec, spec], out_shape=[shp, shp, shp],
        compiler_params=_cparams(("arbitrary",)),
        name="dft_tables",
    )()


def _dft_fwd_kernel(c_ref, s_ref, z_ref, re_ref, im_ref):
    z = z_ref[...]
    re_ref[...] = _dot(c_ref[...], z)
    im_ref[...] = _dot(s_ref[...], z)


def dft_fwd(ctab, stab, zb, tm=512):
    nb, L, cw = zb.shape
    tm = min(tm, L)
    tab = pl.BlockSpec((tm, L), lambda i, b: (i, 0))
    o = pl.BlockSpec((None, tm, cw), lambda i, b: (b, i, 0))
    shp = jax.ShapeDtypeStruct((nb, L, cw), F32)
    return pl.pallas_call(
        _dft_fwd_kernel, grid=(L // tm, nb),
        in_specs=[tab, tab, pl.BlockSpec((None, L, cw), lambda i, b: (b, 0, 0))],
        out_specs=[o, o], out_shape=[shp, shp],
        compiler_params=_cparams(("arbitrary", "arbitrary")),
        name="dft_fwd",
    )(ctab, stab, zb)


def _spec_mul_kernel(xr_ref, xs_ref, kp_ref, kq_ref, yr_ref, yi_ref, *, n2):
    tl = xr_ref.shape[0]
    k = pl.program_id(1) * tl + lax.broadcasted_iota(jnp.int32, xr_ref.shape, 0)
    xr, xs = xr_ref[...], xs_ref[...]
    kre = kp_ref[0] + kp_ref[1]
    kim = kq_ref[1] - kq_ref[0]
    knyq = kq_ref[0] + kq_ref[1]
    first = k == 0
    yre = xr * kre + jnp.where(first, 0.0, xs * kim)
    yim = xr * kim - xs * kre
    yr_ref[...] = (jnp.where(first, 1.0 / n2, 2.0 / n2) * yre).astype(BF16)
    yi_ref[...] = jnp.where(first, xs * knyq * (1.0 / n2), (-2.0 / n2) * yim).astype(BF16)


def spec_mul(xr, xs, kp, kq, tl=512):
    nb, L, cw = xr.shape
    tl = min(tl, L)
    d = pl.BlockSpec((None, tl, cw), lambda b, i: (b, i, 0))
    f = pl.BlockSpec((2, tl, cw), lambda b, i: (0, i, 0))
    shp = jax.ShapeDtypeStruct((nb, L, cw), BF16)
    return pl.pallas_call(
        functools.partial(_spec_mul_kernel, n2=2 * L), grid=(nb, L // tl),
        in_specs=[d, d, f, f], out_specs=[d, d], out_shape=[shp, shp],
        compiler_params=_cparams(("arbitrary", "arbitrary")),
        name="spec_mul",
    )(xr, xs, kp, kq)


def _dft_inv_kernel(c_ref, s_ref, yr_ref, yi_ref, x0_ref, zz_ref, bd_ref, o_ref):
    y = _dot(c_ref[...], yr_ref[...]) + _dot(s_ref[...], yi_ref[...])
    o_ref[...] = (x0_ref[...] * (y + zz_ref[...] * bd_ref[...])).astype(o_ref.dtype)


def dft_inv(ctab, s2tab, yr, yi, x0, zz, bias_d, tm=512):
    nb, L, cw = yr.shape
    tm = min(tm, L)
    nti = L // tm
    tab = pl.BlockSpec((tm, L), lambda i, b: (i, 0))
    full = pl.BlockSpec((None, L, cw), lambda i, b: (b, 0, 0))
    tile = pl.BlockSpec((None, tm, cw), lambda i, b: (b, i, 0))
    return pl.pallas_call(
        _dft_inv_kernel, grid=(nti, nb),
        in_specs=[tab, tab, full, full, tile, tile, pl.BlockSpec((1, cw), lambda i, b: (0, 0))],
        out_specs=pl.BlockSpec((tm, cw), lambda i, b: (b * nti + i, 0)),
        out_shape=jax.ShapeDtypeStruct((nb * L, cw), BF16),
        compiler_params=_cparams(("arbitrary", "arbitrary")),
        name="dft_inv",
    )(ctab, s2tab, yr, yi, x0, zz, bias_d.reshape(1, cw))


def _router_kernel(h_ref, w_ref, b_ref, o_ref):
    logits = _dot_hi(h_ref[...], w_ref[...]) + b_ref[...]
    li = lax.broadcasted_iota(jnp.int32, logits.shape, 1)
    big = jnp.int32(LANES)
    ninf = -jnp.inf

    def rmax(x):
        return jnp.max(x, axis=-1, keepdims=True)

    def first_at(x, m):
        return jnp.min(jnp.where(x == m, li, big), axis=-1, keepdims=True)

    gmask = li < N_GROUPS
    gl = jnp.where(gmask, logits, ninf)
    gmax = rmax(gl)
    g_sel = first_at(gl, gmax)
    g_prob = 1.0 / jnp.sum(jnp.where(gmask, jnp.exp(gl - gmax), 0.0), axis=-1, keepdims=True)
    e = li - N_GROUPS
    emask = (e >= 0) & (e < N_EXPERTS) & ((e >> 2) == g_sel)
    el = jnp.where(emask, logits, ninf)
    emax = rmax(el)
    esum = jnp.sum(jnp.where(emask, jnp.exp(el - emax), 0.0), axis=-1, keepdims=True)
    l1 = first_at(el, emax)
    el2 = jnp.where(li == l1, ninf, el)
    emax2 = rmax(el2)
    l2 = first_at(el2, emax2)
    p1 = 1.0 / esum
    p2 = jnp.exp(emax2 - emax) / esum
    tot = p1 + p2
    w1 = p1 / tot * g_prob
    w2 = p2 / tot * g_prob
    o_ref[...] = jnp.where(li == 0, (l1 - N_GROUPS).astype(F32),
                           jnp.where(li == 1, (l2 - N_GROUPS).astype(F32),
                                     jnp.where(li == 2, w1, jnp.where(li == 3, w2, 0.0))))


def router(h, wg, bg, we, be, tm=512):
    nt, d = h.shape
    wr = _pad2(jnp.concatenate([wg, we], axis=1), d, LANES)
    br = _pad2(jnp.concatenate([bg, be])[None], 1, LANES)
    return pl.pallas_call(
        _router_kernel, grid=(nt // tm,),
        in_specs=[pl.BlockSpec((tm, d), lambda i: (i, 0)),
                  pl.BlockSpec((d, LANES), lambda i: (0, 0)),
                  pl.BlockSpec((1, LANES), lambda i: (0, 0))],
        out_specs=pl.BlockSpec((tm, LANES), lambda i: (i, 0)),
        out_shape=jax.ShapeDtypeStruct((nt, LANES), F32),
        compiler_params=_cparams(("arbitrary",)),
        name="router",
    )(h, wr, br)


def moe_plan(route, tm):
    nt = route.shape[0]
    npair = nt * TOP_K
    ns = npair + N_EXPERTS * tm
    n_tiles = ns // tm
    e = route[:, :TOP_K].astype(jnp.int32).reshape(npair)
    wgt = route[:, TOP_K:2 * TOP_K].reshape(npair)
    onehot = (e[:, None] == jnp.arange(N_EXPERTS, dtype=jnp.int32)[None, :]).astype(jnp.int32)
    csum = jnp.cumsum(onehot, axis=0)
    counts = csum[-1]
    rank = jnp.sum((csum - onehot) * onehot, axis=1)
    padded = ((counts + tm - 1) // tm) * tm
    ends = jnp.cumsum(padded)
    starts = ends - padded
    slot = jnp.sum(onehot * starts[None, :], axis=1) + rank
    pair = jnp.arange(npair, dtype=jnp.int32)
    slot_tok = jnp.zeros((ns,), jnp.int32).at[slot].set(pair // TOP_K)
    slot_dst = jnp.full((ns,), -1, jnp.int32).at[slot].set(pair)
    slot_gate = jnp.zeros((ns,), F32).at[slot].set(wgt)
    tile_start = jnp.arange(n_tiles, dtype=jnp.int32) * tm
    tile_e = jnp.sum((tile_start[:, None] >= ends[None, :]).astype(jnp.int32), axis=1)
    tile_valid = (tile_start < ends[-1]).astype(jnp.int32)
    last_e = jnp.max(jnp.where(counts > 0, jnp.arange(N_EXPERTS, dtype=jnp.int32), 0))
    tile_e = jnp.where(tile_valid > 0, tile_e, last_e).astype(jnp.int32)
    prev = jnp.concatenate([jnp.full((1,), -1, jnp.int32), tile_e[:-1]])
    tile_new = (tile_e != prev).astype(jnp.int32)
    meta = jnp.stack([tile_e, tile_valid, tile_new], axis=0)
    return meta, slot_tok.reshape(n_tiles, 1, tm), slot_dst.reshape(n_tiles, 1, tm), \
        jnp.broadcast_to(slot_gate[:, None], (ns, LANES))


def _moe_kernel(meta_ref, tok_ref, dst_ref, gate_ref, wg_ref, wu_ref, wd_ref, h_hbm, y_hbm,
                xbuf, obuf, wgb, wub, wdb, gsem, ssem):
    t = pl.program_id(0)
    tm = xbuf.shape[0]

    @pl.when(meta_ref[1, t] > 0)
    def _():
        def gather(r):
            return pltpu.make_async_copy(h_hbm.at[pl.ds(tok_ref[0, r], 1)], xbuf.at[pl.ds(r, 1)], gsem)

        def scatter(r):
            return pltpu.make_async_copy(obuf.at[pl.ds(r, 1)], y_hbm.at[pl.ds(dst_ref[0, r], 1)], ssem)

        def start_gather(r, c):
            gather(r).start()
            return c

        lax.fori_loop(0, tm, start_gather, 0)

        @pl.when(meta_ref[2, t] > 0)
        def _():
            wgb[...] = wg_ref[...].astype(BF16)
            wub[...] = wu_ref[...].astype(BF16)
            wdb[...] = wd_ref[...].astype(BF16)

        def wait_gather(r, c):
            gather(r).wait()
            return c

        lax.fori_loop(0, tm, wait_gather, 0)
        x = xbuf[...].astype(BF16)
        a = _dot(x, wgb[...])
        u = _dot(x, wub[...])
        hid = a * (1.0 / (1.0 + jnp.exp(-a))) * u * gate_ref[:, 0:1]
        obuf[...] = _dot(hid.astype(BF16), wdb[...])

        def start_scatter(r, c):
            @pl.when(dst_ref[0, r] >= 0)
            def _():
                scatter(r).start()
            return c

        lax.fori_loop(0, tm, start_scatter, 0)

        def wait_scatter(r, c):
            @pl.when(dst_ref[0, r] >= 0)
            def _():
                scatter(r).wait()
            return c

        lax.fori_loop(0, tm, wait_scatter, 0)


def moe_experts(h, plan, w_gate, w_up, w_down, layer, tm):
    nt, d = h.shape
    de = w_gate.shape[3]
    meta, slot_tok, slot_dst, slot_gate = plan
    n_tiles = slot_tok.shape[0]
    smem_row = lambda: pl.BlockSpec((None, 1, tm), lambda t, m: (t, 0, 0), memory_space=pltpu.SMEM)
    gs = pltpu.PrefetchScalarGridSpec(
        num_scalar_prefetch=1, grid=(n_tiles,),
        in_specs=[smem_row(), smem_row(),
                  pl.BlockSpec((tm, LANES), lambda t, m: (t, 0)),
                  pl.BlockSpec((None, None, d, de), lambda t, m: (layer, m[0, t], 0, 0)),
                  pl.BlockSpec((None, None, d, de), lambda t, m: (layer, m[0, t], 0, 0)),
                  pl.BlockSpec((None, None, de, d), lambda t, m: (layer, m[0, t], 0, 0)),
                  pl.BlockSpec(memory_space=pl.ANY)],
        out_specs=pl.BlockSpec(memory_space=pl.ANY),
        scratch_shapes=[pltpu.VMEM((tm, d), F32), pltpu.VMEM((tm, d), F32),
                        pltpu.VMEM((d, de), BF16), pltpu.VMEM((d, de), BF16), pltpu.VMEM((de, d), BF16),
                        pltpu.SemaphoreType.DMA, pltpu.SemaphoreType.DMA])
    return pl.pallas_call(
        _moe_kernel, grid_spec=gs,
        out_shape=jax.ShapeDtypeStruct((TOP_K * nt, d), F32),
        compiler_params=_cparams(("arbitrary",)),
        name="moe_experts",
    )(meta, slot_tok, slot_dst, slot_gate, w_gate, w_up, w_down, h)


def kernel(x_prompt, x_sample, c, cache_na_k, cache_na_v, cache_gqa_k, cache_gqa_v, c_ctx, norm1_g, norm2_g, final_norm_g, ada_w, ada_b, w_in, w_out, na_rpb, q_norm_g, k_norm_g, hy_conv_w, hy_conv_b, hy_w1, hy_b1, hy_w2, hy_b2, hy_w3, hy_b3, hy_freq, hy_w_out, hy_bias, router_group_w, router_group_b, router_expert_w, router_expert_b, moe_w_gate, moe_w_up, moe_w_down):
    bc, lc, d = x_prompt.shape
    bl, ll, _ = x_sample.shape
    depth = w_in.shape[0]
    past = cache_na_k.shape[2]
    dh = HEAD_DIM
    na_w, gq_w = d // 4, d // 2
    hy_w = d - na_w - gq_w
    na_heads = na_w // dh
    nc = bc * lc
    nt = nc + bl * ll
    tm = 512
    moe_tm = 256
    assert nc % tm == 0 and ll % tm == 0 and past == lc

    te = 256

    def seg_of_tiles(t):
        return jnp.asarray(np.concatenate(
            [np.zeros(nc // t), 1 + np.repeat(np.arange(bl), ll // t)]).astype(np.int32))

    seg_ids = seg_of_tiles(tm)
    seg_e = seg_of_tiles(te)
    nseg = 1 + bl
    cond = jnp.zeros((8, d), F32).at[0].set(c_ctx).at[1:1 + bl].set(c)
    mods_all = modulation_all(cond, ada_w, ada_b)

    ropes = rope_tables(ll)
    tabs_c = dft_tables(lc)
    tabs_l = dft_tables(ll)
    ck_na = cache_na_k.reshape(bl, depth, past, na_w)
    cv_na = cache_na_v.reshape(bl, depth, past, na_w)
    ck_gq = cache_gqa_k.reshape(bl, depth, past, GQA_KV_HEADS * dh)
    cv_gq = cache_gqa_v.reshape(bl, depth, past, GQA_KV_HEADS * dh)

    x = jnp.concatenate([x_prompt.reshape(nc, d), x_sample.reshape(bl * ll, d)], axis=0)
    mods = mods_all[0, :nseg].reshape(nseg, 1, 6 * d)
    h = norm_mod(x, norm1_g[0], mods, 1, 0, seg_e, te, BF16)
    st_na_k, st_na_v, st_gq_k, st_gq_v = [], [], [], []
    hy_col0 = 3 * na_w + gq_w + 2 * GQA_KV_HEADS * dh
    for l in range(depth):
        z = in_proj(h, w_in, l)
        zc = z[:nc]
        st_na_k.append(zc[:, na_w:2 * na_w].reshape(bc, lc, na_heads, dh))
        st_na_v.append(zc[:, 2 * na_w:3 * na_w].reshape(bc, lc, na_heads, dh))
        st_gq_v.append(zc[:, 3 * na_w + gq_w + GQA_KV_HEADS * dh:hy_col0].reshape(bc, lc, GQA_KV_HEADS, dh))

        ctx_a, ctx_b, bk = ctx_attn(z, q_norm_g[l], k_norm_g[l], bc, lc, na_w, gq_w)
        st_gq_k.append(bk.reshape(bc, lc, GQA_KV_HEADS, dh))

        bias = na_bias(na_rpb[l])
        lat_a = lat_na(z, ck_na, cv_na, l, bias, nc, bl, ll, na_w)
        q, k, v = lat_prep(z, ropes, q_norm_g[l], k_norm_g[l], nc, bl, ll, gq_w)
        lat_b = lat_gqa(q, k, v, ck_gq, cv_gq, l)

        mix_c = []
        for (row0, nb, L, tabs) in ((0, bc, lc, tabs_c), (nc, bl, ll, tabs_l)):
            ctab, s1tab, s2tab = tabs
            x0, zz, zb = hy_pre(z, hy_conv_w[l], hy_conv_b[l], row0, nb, L, hy_col0, hy_w)
            filt = hy_filter(L, hy_w1[l], hy_b1[l], hy_w2[l], hy_b2[l], hy_w3[l], hy_b3[l],
                             hy_freq[l], hy_w_out[l])
            kp, kq = dft_fwd(ctab, s1tab, filt)
            xr, xs = dft_fwd(ctab, s1tab, zb)
            yr, yi = spec_mul(xr, xs, kp, kq)
            mix_c.append(dft_inv(ctab, s2tab, yr, yi, x0, zz, hy_bias[l]))

        x1 = out_proj((ctx_a, ctx_b, mix_c[0]), (lat_a, lat_b, mix_c[1]), w_out, l, x, mods, seg_ids)
        h2 = norm_mod(x1, norm2_g[l], mods, 4, 3, seg_e, te, F32)
        route = router(h2, router_group_w[l], router_group_b[l], router_expert_w[l], router_expert_b[l])
        plan = moe_plan(route, moe_tm)
        y = moe_experts(h2, plan, moe_w_gate, moe_w_up, moe_w_down, l, moe_tm)
        y2 = y.reshape(nt, TOP_K * d)
        last = l == depth - 1
        mods_next = mods if last else mods_all[l + 1, :nseg].reshape(nseg, 1, 6 * d)
        g_next = final_norm_g if last else norm1_g[l + 1]
        x, h = combine_norm(x1, y2, g_next, mods, mods_next, seg_e, te, not last, F32 if last else BF16)
        mods = mods_next

    y_prompt = h[:nc].reshape(bc, lc, d)
    y_sample = h[nc:].reshape(bl, ll, d)
    return (y_prompt, y_sample, jnp.stack(st_na_k, axis=1), jnp.stack(st_na_v, axis=1),
            jnp.stack(st_gq_k, axis=1), jnp.stack(st_gq_v, axis=1))
```

```python
import functools
import math

import numpy as np
import jax
import jax.numpy as jnp
from jax import lax
from jax.experimental import pallas as pl
from jax.experimental.pallas import tpu as pltpu

F32 = jnp.float32
BF16 = jnp.bfloat16
HIGHEST = lax.Precision.HIGHEST

GRID_W = 64
HEAD_DIM = 128
NA_WIN_R = 8
NA_WIN_C = 16
GQA_KV_HEADS = 2
ROPE_THETA = 10000.0
HY_EMB = 33
HY_HID = 64
HY_FAST_DECAY = 0.3
HY_SLOW_DECAY = 1.5
HY_TARGET = 0.01
N_GROUPS = 4
EXP_PER_GROUP = 4
N_EXPERTS = N_GROUPS * EXP_PER_GROUP
TOP_K = 2
EPS = 1e-6
NEG_INF = -1e30

LANES = 128
VMEM_LIMIT = 52 * 1024 * 1024


def _cparams(sem, vmem=VMEM_LIMIT):
    return pltpu.CompilerParams(dimension_semantics=sem, vmem_limit_bytes=vmem)


def _dot(a, b):
    return jnp.dot(a, b, preferred_element_type=F32)


def _dot_t(a, b):
    return lax.dot_general(a, b, (((1,), (1,)), ((), ())), preferred_element_type=F32)


def _dot_hi(a, b):
    return jnp.dot(a, b, preferred_element_type=F32, precision=HIGHEST)


def _mod_kernel(c_ref, w_ref, b_ref, o_ref):
    c = c_ref[...]
    s = c * (1.0 / (1.0 + jnp.exp(-c)))
    o_ref[...] = _dot_hi(s, w_ref[...]) + b_ref[...]


def modulation_all(cond, ada_w, ada_b, tn=1536):
    depth, d, n = ada_w.shape
    r = cond.shape[0]
    return pl.pallas_call(
        _mod_kernel,
        grid=(depth, n // tn),
        in_specs=[pl.BlockSpec((r, d), lambda l, j: (0, 0)),
                  pl.BlockSpec((None, d, tn), lambda l, j: (l, 0, j)),
                  pl.BlockSpec((None, 1, tn), lambda l, j: (l, 0, j))],
        out_specs=pl.BlockSpec((None, r, tn), lambda l, j: (l, 0, j)),
        out_shape=jax.ShapeDtypeStruct((depth, r, n), F32),
        compiler_params=_cparams(("arbitrary", "arbitrary")),
        name="modulation",
    )(cond, ada_w, ada_b.reshape(depth, 1, n))


def _rms(x):
    return x * lax.rsqrt(jnp.mean(x * x, axis=-1, keepdims=True) + EPS)


def _norm_mod_kernel(s_ref, x_ref, g_ref, sc_ref, sh_ref, o_ref):
    h = _rms(x_ref[...]) * g_ref[...]
    o_ref[...] = (h * (1.0 + sc_ref[...]) + sh_ref[...]).astype(o_ref.dtype)


def norm_mod(x, g, mods, sc_idx, sh_idx, seg_ids, tm, out_dtype):
    nt, d = x.shape
    gs = pltpu.PrefetchScalarGridSpec(
        num_scalar_prefetch=1, grid=(nt // tm,),
        in_specs=[pl.BlockSpec((tm, d), lambda i, s: (i, 0)),
                  pl.BlockSpec((1, d), lambda i, s: (0, 0)),
                  pl.BlockSpec((None, 1, d), lambda i, s: (s[i], 0, sc_idx)),
                  pl.BlockSpec((None, 1, d), lambda i, s: (s[i], 0, sh_idx))],
        out_specs=pl.BlockSpec((tm, d), lambda i, s: (i, 0)))
    return pl.pallas_call(
        _norm_mod_kernel, grid_spec=gs,
        out_shape=jax.ShapeDtypeStruct((nt, d), out_dtype),
        compiler_params=_cparams(("arbitrary",)),
        name="norm_mod",
    )(seg_ids, x, g.reshape(1, d), mods, mods)


def _mm_kernel(a_ref, w_ref, o_ref, wbf_ref):
    @pl.when(pl.program_id(1) == 0)
    def _():
        wbf_ref[...] = w_ref[...].astype(BF16)

    o_ref[...] = _dot(a_ref[...], wbf_ref[...])


def in_proj(h, w_in, layer, tm=512, tn=768):
    nt, d = h.shape
    n = w_in.shape[2]
    return pl.pallas_call(
        _mm_kernel,
        grid=(n // tn, nt // tm),
        in_specs=[pl.BlockSpec((tm, d), lambda j, i: (i, 0)),
                  pl.BlockSpec((None, d, tn), lambda j, i: (layer, 0, j))],
        out_specs=pl.BlockSpec((tm, tn), lambda j, i: (i, j)),
        out_shape=jax.ShapeDtypeStruct((nt, n), F32),
        scratch_shapes=[pltpu.VMEM((d, tn), BF16)],
        compiler_params=_cparams(("arbitrary", "arbitrary")),
        name="in_proj",
    )(h, w_in)


def _out_proj_kernel(s_ref, ca_ref, cb_ref, cc_ref, la_ref, lb_ref, lc_ref, w_ref, x_ref, g1_ref, o_ref,
                     wbf_ref, *, nci):
    i = pl.program_id(1)

    @pl.when(i == 0)
    def _():
        wbf_ref[...] = w_ref[...].astype(BF16)

    def project(a_ref, b_ref, c_ref):
        wa = a_ref.shape[1]
        wb = b_ref.shape[1]
        acc = _dot(a_ref[...], wbf_ref[0:wa, :])
        acc += _dot(b_ref[...], wbf_ref[wa:wa + wb, :])
        acc += _dot(c_ref[...], wbf_ref[wa + wb:, :])
        o_ref[...] = x_ref[...] + g1_ref[...] * acc

    @pl.when(i < nci)
    def _():
        project(ca_ref, cb_ref, cc_ref)

    @pl.when(i >= nci)
    def _():
        project(la_ref, lb_ref, lc_ref)


def out_proj(mix_ctx, mix_lat, w_out, layer, x, mods, seg_ids, tm=512, tn=512):
    nt, d = x.shape
    nci = mix_ctx[0].shape[0] // tm
    nj = d // tn
    cspec = lambda a: pl.BlockSpec((tm, a.shape[1]), lambda j, i, s: (jnp.minimum(i, nci - 1), 0))
    lspec = lambda a: pl.BlockSpec((tm, a.shape[1]), lambda j, i, s: (jnp.maximum(i - nci, 0), 0))
    gs = pltpu.PrefetchScalarGridSpec(
        num_scalar_prefetch=1, grid=(nj, nt // tm),
        in_specs=[cspec(mix_ctx[0]), cspec(mix_ctx[1]), cspec(mix_ctx[2]),
                  lspec(mix_lat[0]), lspec(mix_lat[1]), lspec(mix_lat[2]),
                  pl.BlockSpec((None, d, tn), lambda j, i, s: (layer, 0, j)),
                  pl.BlockSpec((tm, tn), lambda j, i, s: (i, j)),
                  pl.BlockSpec((None, 1, tn), lambda j, i, s: (s[i], 0, 2 * nj + j))],
        out_specs=pl.BlockSpec((tm, tn), lambda j, i, s: (i, j)),
        scratch_shapes=[pltpu.VMEM((d, tn), BF16)])
    return pl.pallas_call(
        functools.partial(_out_proj_kernel, nci=nci), grid_spec=gs,
        out_shape=jax.ShapeDtypeStruct((nt, d), F32),
        compiler_params=_cparams(("arbitrary", "arbitrary")),
        name="out_proj",
    )(seg_ids, *mix_ctx, *mix_lat, w_out, x, mods)


def _softmax_pv(s, v):
    m = jnp.max(s, axis=-1, keepdims=True)
    p = jnp.exp(s - m)
    l = jnp.sum(p, axis=-1, keepdims=True)
    return _dot(p.astype(BF16), v) / l


def _ctx_attn_kernel(za_ref, zb_ref, qg_ref, kg_ref, a_ref, b_ref, bk_ref, *, na_heads, gq_heads):
    dh = HEAD_DIM
    scale = dh ** -0.5
    L = za_ref.shape[0]
    na_w = na_heads * dh
    for h in range(na_heads):
        q = za_ref[:, h * dh:(h + 1) * dh].astype(BF16)
        k = za_ref[:, na_w + h * dh:na_w + (h + 1) * dh].astype(BF16)
        v = za_ref[:, 2 * na_w + h * dh:2 * na_w + (h + 1) * dh].astype(BF16)
        s = _dot_t(q, k) * scale
        a_ref[:, h * dh:(h + 1) * dh] = _softmax_pv(s, v).astype(a_ref.dtype)
    gq_w = gq_heads * dh
    group = gq_heads // GQA_KV_HEADS
    for kv in range(GQA_KV_HEADS):
        kn = _rms(zb_ref[:, gq_w + kv * dh:gq_w + (kv + 1) * dh]) * kg_ref[...]
        bk_ref[:, kv * dh:(kv + 1) * dh] = kn
        knb = kn.astype(BF16)
        v = zb_ref[:, gq_w + (GQA_KV_HEADS + kv) * dh:gq_w + (GQA_KV_HEADS + kv + 1) * dh].astype(BF16)
        for g in range(group):
            h = kv * group + g
            qn = (_rms(zb_ref[:, h * dh:(h + 1) * dh]) * qg_ref[...]).astype(BF16)
            s = _dot_t(qn, knb) * scale
            b_ref[:, h * dh:(h + 1) * dh] = _softmax_pv(s, v).astype(b_ref.dtype)


def ctx_attn(z, q_g, k_g, bc, lc, na_w, gq_w):
    dh = HEAD_DIM
    wa = 3 * na_w
    wb = gq_w + 2 * GQA_KV_HEADS * dh
    assert wa == wb
    kern = functools.partial(_ctx_attn_kernel, na_heads=na_w // dh, gq_heads=gq_w // dh)
    return pl.pallas_call(
        kern, grid=(bc,),
        in_specs=[pl.BlockSpec((lc, wa), lambda b: (b, 0)),
                  pl.BlockSpec((lc, wb), lambda b: (b, 1)),
                  pl.BlockSpec((1, dh), lambda b: (0, 0)),
                  pl.BlockSpec((1, dh), lambda b: (0, 0))],
        out_specs=[pl.BlockSpec((lc, na_w), lambda b: (b, 0)),
                   pl.BlockSpec((lc, gq_w), lambda b: (b, 0)),
                   pl.BlockSpec((lc, GQA_KV_HEADS * dh), lambda b: (b, 0))],
        out_shape=[jax.ShapeDtypeStruct((bc * lc, na_w), BF16),
                   jax.ShapeDtypeStruct((bc * lc, gq_w), BF16),
                   jax.ShapeDtypeStruct((bc * lc, GQA_KV_HEADS * dh), F32)],
        compiler_params=_cparams(("arbitrary",)),
        name="ctx_attn",
    )(z, z, q_g.reshape(1, dh), k_g.reshape(1, dh))


def _rope_kernel(cos_ref, sa_ref, sb_ref):
    n, dh = cos_ref.shape
    quarter = dh // 2
    t = lax.broadcasted_iota(jnp.int32, (n, dh), 0)
    lane = lax.broadcasted_iota(jnp.int32, (n, dh), 1)
    f = (lane & (quarter // 2 - 1)).astype(F32)
    inv = jnp.exp(-(2.0 * f / quarter) * math.log(ROPE_THETA))
    pos = jnp.where(lane < quarter, t >> int(math.log2(GRID_W)), t & (GRID_W - 1)).astype(F32)
    ang = pos * inv
    sin = jnp.sin(ang)
    first = (lane & (quarter - 1)) < (quarter // 2)
    cos_ref[...] = jnp.cos(ang)
    sa_ref[...] = jnp.where(first, -sin, 0.0)
    sb_ref[...] = jnp.where(first, 0.0, sin)


def rope_tables(n):
    shp = jax.ShapeDtypeStruct((n, HEAD_DIM), F32)
    return pl.pallas_call(_rope_kernel, out_shape=[shp, shp, shp], name="rope_tables")()


def _rope(x, cos, sa, sb):
    q4 = HEAD_DIM // 4
    return x * cos + pltpu.roll(x, HEAD_DIM - q4, 1) * sa + pltpu.roll(x, q4, 1) * sb


def _lat_prep_kernel(z_ref, cos_ref, sa_ref, sb_ref, qg_ref, kg_ref, q_ref, k_ref, v_ref, *, gq_heads):
    dh = HEAD_DIM
    scale = dh ** -0.5
    cos, sa, sb = cos_ref[...], sa_ref[...], sb_ref[...]
    for h in range(gq_heads):
        qn = _rms(z_ref[:, h * dh:(h + 1) * dh]) * qg_ref[...]
        q_ref[h] = (_rope(qn, cos, sa, sb) * scale).astype(BF16)
    gq_w = gq_heads * dh
    for kv in range(GQA_KV_HEADS):
        kn = _rms(z_ref[:, gq_w + kv * dh:gq_w + (kv + 1) * dh]) * kg_ref[...]
        k_ref[kv] = _rope(kn, cos, sa, sb).astype(BF16)
        v_ref[kv] = z_ref[:, gq_w + (GQA_KV_HEADS + kv) * dh:gq_w + (GQA_KV_HEADS + kv + 1) * dh].astype(BF16)


def lat_prep(z, ropes, q_g, k_g, nc, bl, ll, gq_w, tl=512):
    dh = HEAD_DIM
    gq_heads = gq_w // dh
    wb = gq_w + 2 * GQA_KV_HEADS * dh
    nb = ll // tl
    off = nc // tl
    cos, sa, sb = ropes
    tab = pl.BlockSpec((tl, dh), lambda b, i: (i, 0))
    one = pl.BlockSpec((1, dh), lambda b, i: (0, 0))
    return pl.pallas_call(
        functools.partial(_lat_prep_kernel, gq_heads=gq_heads),
        grid=(bl, nb),
        in_specs=[pl.BlockSpec((tl, wb), lambda b, i: (off + b * nb + i, 1)), tab, tab, tab, one, one],
        out_specs=[pl.BlockSpec((None, gq_heads, tl, dh), lambda b, i: (b, 0, i, 0)),
                   pl.BlockSpec((None, GQA_KV_HEADS, tl, dh), lambda b, i: (b, 0, i, 0)),
                   pl.BlockSpec((None, GQA_KV_HEADS, tl, dh), lambda b, i: (b, 0, i, 0))],
        out_shape=[jax.ShapeDtypeStruct((bl, gq_heads, ll, dh), BF16),
                   jax.ShapeDtypeStruct((bl, GQA_KV_HEADS, ll, dh), BF16),
                   jax.ShapeDtypeStruct((bl, GQA_KV_HEADS, ll, dh), BF16)],
        compiler_params=_cparams(("arbitrary", "arbitrary")),
        name="lat_prep",
    )(z, cos, sa, sb, q_g.reshape(1, dh), k_g.reshape(1, dh))


def _lat_gqa_kernel(q_ref, k_ref, v_ref, ck_ref, cv_ref, o_ref, *, tk):
    group, tq, dh = q_ref.shape
    q = q_ref[...].reshape(group * tq, dh)
    ll = k_ref.shape[0]

    def update(carry, kc, vc):
        m, l, acc = carry
        s = _dot_t(q, kc)
        m_new = jnp.maximum(m, jnp.max(s, axis=-1, keepdims=True))
        alpha = jnp.exp(m - m_new)
        p = jnp.exp(s - m_new)
        l = alpha * l + jnp.sum(p, axis=-1, keepdims=True)
        acc = alpha * acc + _dot(p.astype(BF16), vc)
        return m_new, l, acc

    def body(c, carry):
        st = pl.multiple_of(c * tk, tk)
        return update(carry, k_ref[pl.ds(st, tk), :], v_ref[pl.ds(st, tk), :])

    init = (jnp.full((group * tq, 1), -jnp.inf, F32), jnp.zeros((group * tq, 1), F32),
            jnp.zeros((group * tq, dh), F32))
    carry = lax.fori_loop(0, ll // tk, body, init)
    m, l, acc = update(carry, ck_ref[...].astype(BF16), cv_ref[...].astype(BF16))
    out = acc / l
    for g in range(group):
        o_ref[:, g * dh:(g + 1) * dh] = out[g * tq:(g + 1) * tq].astype(o_ref.dtype)


def lat_gqa(q, k, v, cache_k, cache_v, layer, tq=256, tk=512):
    bl, gq_heads, ll, dh = q.shape
    group = gq_heads // GQA_KV_HEADS
    past = cache_k.shape[2]
    nq = ll // tq
    tk = min(tk, ll)
    ck = pl.BlockSpec((None, None, past, dh), lambda b, h, i: (b, layer, 0, h))
    return pl.pallas_call(
        functools.partial(_lat_gqa_kernel, tk=tk),
        grid=(bl, GQA_KV_HEADS, nq),
        in_specs=[pl.BlockSpec((None, group, tq, dh), lambda b, h, i: (b, h, i, 0)),
                  pl.BlockSpec((None, None, ll, dh), lambda b, h, i: (b, h, 0, 0)),
                  pl.BlockSpec((None, None, ll, dh), lambda b, h, i: (b, h, 0, 0)),
                  ck, ck],
        out_specs=pl.BlockSpec((tq, group * dh), lambda b, h, i: (b * nq + i, h)),
        out_shape=jax.ShapeDtypeStruct((bl * ll, gq_heads * dh), BF16),
        compiler_params=_cparams(("arbitrary", "arbitrary", "arbitrary")),
        name="lat_gqa",
    )(q, k, v, cache_k, cache_v)


def _na_bias_kernel(rpb_ref, o_ref):
    h = pl.program_id(0)
    w = GRID_W
    qc = lax.broadcasted_iota(jnp.int32, (w, w), 0)
    kc = lax.broadcasted_iota(jnp.int32, (w, w), 1)
    ci = jnp.clip(kc - qc + NA_WIN_C - 1, 0, 2 * NA_WIN_C - 2)
    cs = jnp.clip(qc - NA_WIN_C // 2, 0, w - NA_WIN_C)
    col_ok = (kc >= cs) & (kc < cs + NA_WIN_C)
    nr, ncol = 2 * NA_WIN_R - 1, 2 * NA_WIN_C - 1
    tiles = []
    for dr in range(nr):
        t = jnp.zeros((w, w), F32)
        for c in range(ncol):
            t = jnp.where(ci == c, rpb_ref[h, dr * ncol + c], t)
        tiles.append(jnp.where(col_ok, t, NEG_INF))
    for d0 in range(NA_WIN_R):
        for j in range(NA_WIN_R):
            o_ref[d0, :, j * w:(j + 1) * w] = tiles[d0 + j]


def na_bias(rpb_l):
    heads = rpb_l.shape[0]
    flat = rpb_l.reshape(heads, -1)
    return pl.pallas_call(
        _na_bias_kernel, grid=(heads,),
        in_specs=[pl.BlockSpec(memory_space=pltpu.SMEM)],
        out_specs=pl.BlockSpec((None, NA_WIN_R, GRID_W, NA_WIN_R * GRID_W), lambda h: (h, 0, 0, 0)),
        out_shape=jax.ShapeDtypeStruct((heads, NA_WIN_R, GRID_W, NA_WIN_R * GRID_W), F32),
        compiler_params=_cparams(("arbitrary",)),
        name="na_bias",
    )(flat)


def _lat_na_kernel(q_ref, k_ref, v_ref, ck_ref, cv_ref, bias_ref, o_ref, kb_ref, vb_ref, *, rb, rows):
    w = GRID_W
    scale = HEAD_DIM ** -0.5
    i = pl.program_id(2)

    @pl.when(i == 0)
    def _():
        kb_ref[...] = k_ref[...].astype(BF16)
        vb_ref[...] = v_ref[...].astype(BF16)

    ck = ck_ref[...].astype(BF16)
    cv = cv_ref[...].astype(BF16)
    for j in range(rb):
        r = i * rb + j
        rs = jnp.clip(r - NA_WIN_R // 2, 0, rows - NA_WIN_R)
        d0 = rs - r + NA_WIN_R - 1
        st = pl.multiple_of(rs * w, w)
        q = q_ref[j * w:(j + 1) * w, :].astype(BF16)
        kw = kb_ref[pl.ds(st, NA_WIN_R * w), :]
        vw = vb_ref[pl.ds(st, NA_WIN_R * w), :]
        s1 = _dot_t(q, kw) * scale + bias_ref[d0]
        s2 = _dot_t(q, ck) * scale
        m = jnp.maximum(jnp.max(s1, axis=-1, keepdims=True), jnp.max(s2, axis=-1, keepdims=True))
        p1 = jnp.exp(s1 - m)
        p2 = jnp.exp(s2 - m)
        l = jnp.sum(p1, axis=-1, keepdims=True) + jnp.sum(p2, axis=-1, keepdims=True)
        o = (_dot(p1.astype(BF16), vw) + _dot(p2.astype(BF16), cv)) / l
        o_ref[j * w:(j + 1) * w, :] = o.astype(o_ref.dtype)


def lat_na(z, cache_k, cache_v, layer, bias, nc, bl, ll, na_w, rb=8):
    dh = HEAD_DIM
    heads = na_w // dh
    rows = ll // GRID_W
    assert rows >= NA_WIN_R and nc % ll == 0 and rows % rb == 0
    past = cache_k.shape[2]
    nb = rows // rb
    tq = rb * GRID_W
    off = nc // tq
    lat0 = nc // ll
    ck = pl.BlockSpec((None, None, past, dh), lambda b, h, i: (b, layer, 0, h))
    return pl.pallas_call(
        functools.partial(_lat_na_kernel, rb=rb, rows=rows),
        grid=(bl, heads, nb),
        in_specs=[pl.BlockSpec((tq, dh), lambda b, h, i: (off + b * nb + i, h)),
                  pl.BlockSpec((ll, dh), lambda b, h, i: (lat0 + b, heads + h)),
                  pl.BlockSpec((ll, dh), lambda b, h, i: (lat0 + b, 2 * heads + h)),
                  ck, ck,
                  pl.BlockSpec((None, NA_WIN_R, GRID_W, NA_WIN_R * GRID_W), lambda b, h, i: (h, 0, 0, 0))],
        out_specs=pl.BlockSpec((tq, dh), lambda b, h, i: (b * nb + i, h)),
        out_shape=jax.ShapeDtypeStruct((bl * ll, na_w), BF16),
        scratch_shapes=[pltpu.VMEM((ll, dh), BF16), pltpu.VMEM((ll, dh), BF16)],
        compiler_params=_cparams(("arbitrary", "arbitrary", "arbitrary")),
        name="lat_na",
    )(z, z, z, cache_k, cache_v, bias)


def _hy_pre_kernel(u0_ref, u1_ref, u2_ref, w0_ref, w1_ref, w2_ref, b0_ref, b1_ref, b2_ref,
                   x0_ref, zz_ref, zb_ref):
    L = u0_ref.shape[0]
    t = lax.broadcasted_iota(jnp.int32, u0_ref.shape, 0)

    def conv(u_ref, w_ref, b_ref):
        u = u_ref[...]
        prev = jnp.where(t == 0, 0.0, pltpu.roll(u, 1, 0))
        nxt = jnp.where(t == L - 1, 0.0, pltpu.roll(u, L - 1, 0))
        return prev * w_ref[0:1, :] + u * w_ref[1:2, :] + nxt * w_ref[2:3, :] + b_ref[...]

    x0_ref[...] = conv(u0_ref, w0_ref, b0_ref)
    zz = conv(u2_ref, w2_ref, b2_ref) * conv(u1_ref, w1_ref, b1_ref)
    zz_ref[...] = zz
    zb_ref[...] = zz.astype(BF16)


def hy_pre(z, conv_w, conv_b, row0, nb, L, col0, hy_w):
    cb = LANES
    nj = hy_w // cb
    c0 = col0 // cb
    r0 = row0 // L
    assert row0 % L == 0
    u = lambda part: pl.BlockSpec((L, cb), lambda b, j: (r0 + b, c0 + part * nj + j))
    wspec = lambda part: pl.BlockSpec((3, cb), lambda b, j: (0, part * nj + j))
    bspec = lambda part: pl.BlockSpec((1, cb), lambda b, j: (0, part * nj + j))
    o = pl.BlockSpec((None, L, cb), lambda b, j: (b, 0, j))
    return pl.pallas_call(
        _hy_pre_kernel, grid=(nb, nj),
        in_specs=[u(0), u(1), u(2), wspec(0), wspec(1), wspec(2), bspec(0), bspec(1), bspec(2)],
        out_specs=[o, o, o],
        out_shape=[jax.ShapeDtypeStruct((nb, L, hy_w), F32), jax.ShapeDtypeStruct((nb, L, hy_w), F32),
                   jax.ShapeDtypeStruct((nb, L, hy_w), BF16)],
        compiler_params=_cparams(("arbitrary", "arbitrary")),
        name="hy_pre",
    )(z, z, z, conv_w, conv_w, conv_w, conv_b.reshape(1, -1), conv_b.reshape(1, -1), conv_b.reshape(1, -1))


def _hy_filter_kernel(w1_ref, b1_ref, w2_ref, b2_ref, w3_ref, b3_ref, fq_ref, wo_ref, o_ref, *, hy_w):
    L = o_ref.shape[1]
    bands = (HY_EMB - 1) // 2
    ti = lax.broadcasted_iota(jnp.int32, (L, LANES), 0).astype(F32)
    lane = lax.broadcasted_iota(jnp.int32, (L, LANES), 1)
    t01 = ti / (L - 1)
    w = (2.0 * math.pi / L) * ti
    band = ((lane - 1) % bands).astype(F32)
    fr = 1e-4 + band * ((bands - 1 - 1e-4) / (bands - 1))
    ang = fr * w
    feat = jnp.where(lane == 0, t01,
                     jnp.where(lane <= bands, jnp.cos(ang),
                               jnp.where(lane <= 2 * bands, -jnp.sin(ang), 0.0)))
    fq = fq_ref[...]
    hdn = jnp.sin(fq * (_dot_hi(feat, w1_ref[...]) + b1_ref[...]))
    hdn = jnp.sin(fq * (_dot_hi(hdn, w2_ref[...]) + b2_ref[...]))
    hdn = jnp.sin(fq * (_dot_hi(hdn, w3_ref[...]) + b3_ref[...]))
    filt = _dot_hi(hdn, wo_ref[...])
    max_decay = math.log(HY_TARGET) / HY_FAST_DECAY
    min_decay = math.log(HY_TARGET) / HY_SLOW_DECAY
    ch = lax.broadcasted_iota(jnp.int32, (L, hy_w), 1).astype(F32)
    deltas = jnp.abs(min_decay + ch * ((max_decay - min_decay) / (hy_w - 1)))
    tt = lax.broadcasted_iota(jnp.int32, (L, hy_w), 0)
    decay = jnp.exp(-(tt.astype(F32) / (L - 1)) * deltas)
    o_ref[0] = (filt[:, :hy_w] * decay).astype(o_ref.dtype)
    o_ref[1] = jnp.where(tt == 0, 0.0, filt[:, hy_w:] * decay).astype(o_ref.dtype)


def _pad2(a, r, c):
    return jnp.pad(a, ((0, r - a.shape[0]), (0, c - a.shape[1])))


def hy_filter(L, w1, b1, w2, b2, w3, b3, freq, w_out):
    hy_w = w_out.shape[1] // 2
    p = LANES
    args = (_pad2(w1, p, p), _pad2(b1[None], 1, p), _pad2(w2, p, p), _pad2(b2[None], 1, p),
            _pad2(w3, p, p), _pad2(b3[None], 1, p), _pad2(freq[None], 1, p), _pad2(w_out, p, 2 * hy_w))
    return pl.pallas_call(
        functools.partial(_hy_filter_kernel, hy_w=hy_w),
        out_shape=jax.ShapeDtypeStruct((2, L, hy_w), BF16),
        compiler_params=_cparams(None),
        name="hy_filter",
    )(*args)


def _dft_tables_kernel(c_ref, s1_ref, s2_ref, *, L):
    tk = c_ref.shape[0]
    n2 = 2 * L
    theta = 2.0 * math.pi / n2
    k = pl.program_id(0) * tk + lax.broadcasted_iota(jnp.int32, (tk, LANES), 0)
    lane = lax.broadcasted_iota(jnp.int32, (tk, LANES), 1)
    a = ((k * lane) & (n2 - 1)).astype(F32) * theta
    b = ((k * LANES * lane) & (n2 - 1)).astype(F32) * theta
    ca, sa, cb, sb = jnp.cos(a), jnp.sin(a), jnp.cos(b), jnp.sin(b)
    sign_k = (1 - 2 * (k & 1)).astype(F32)
    for j in range(L // LANES):
        cbj = cb[:, j:j + 1]
        sbj = sb[:, j:j + 1]
        c = ca * cbj - sa * sbj
        s = sa * cbj + ca * sbj
        t = lane + j * LANES
        sign_t = (1 - 2 * (t & 1)).astype(F32)
        c_ref[:, j * LANES:(j + 1) * LANES] = c.astype(BF16)
        s1_ref[:, j * LANES:(j + 1) * LANES] = jnp.where(k == 0, sign_t, s).astype(BF16)
        s2_ref[:, j * LANES:(j + 1) * LANES] = jnp.where(t == 0, sign_k, s).astype(BF16)


def dft_tables(L, tk=256):
    tk = min(tk, L)
    shp = jax.ShapeDtypeStruct((L, L), BF16)
    spec = pl.BlockSpec((tk, L), lambda i: (i, 0))
    return pl.pallas_call(
        functools.partial(_dft_tables_kernel, L=L), grid=(L // tk,),
        out_specs=[spec, spec, spec], out_shape=[shp, shp, shp],
        compiler_params=_cparams(("arbitrary",)),
        name="dft_tables",
    )()


def _dft_fwd_kernel(c_ref, s_ref, z_ref, re_ref, im_ref):
    z = z_ref[...]
    re_ref[...] = _dot(c_ref[...], z)
    im_ref[...] = _dot(s_ref[...], z)


def dft_fwd(ctab, stab, zb, tm=512):
    nb, L, cw = zb.shape
    tm = min(tm, L)
    tab = pl.BlockSpec((tm, L), lambda i, b: (i, 0))
    o = pl.BlockSpec((None, tm, cw), lambda i, b: (b, i, 0))
    shp = jax.ShapeDtypeStruct((nb, L, cw), F32)
    return pl.pallas_call(
        _dft_fwd_kernel, grid=(L // tm, nb),
        in_specs=[tab, tab, pl.BlockSpec((None, L, cw), lambda i, b: (b, 0, 0))],
        out_specs=[o, o], out_shape=[shp, shp],
        compiler_params=_cparams(("arbitrary", "arbitrary")),
        name="dft_fwd",
    )(ctab, stab, zb)


def _spec_mul_kernel(xr_ref, xs_ref, kp_ref, kq_ref, yr_ref, yi_ref, *, n2):
    tl = xr_ref.shape[0]
    k = pl.program_id(1) * tl + lax.broadcasted_iota(jnp.int32, xr_ref.shape, 0)
    xr, xs = xr_ref[...], xs_ref[...]
    kre = kp_ref[0] + kp_ref[1]
    kim = kq_ref[1] - kq_ref[0]
    knyq = kq_ref[0] + kq_ref[1]
    first = k == 0
    yre = xr * kre + jnp.where(first, 0.0, xs * kim)
    yim = xr * kim - xs * kre
    yr_ref[...] = (jnp.where(first, 1.0 / n2, 2.0 / n2) * yre).astype(BF16)
    yi_ref[...] = jnp.where(first, xs * knyq * (1.0 / n2), (-2.0 / n2) * yim).astype(BF16)


def spec_mul(xr, xs, kp, kq, tl=512):
    nb, L, cw = xr.shape
    tl = min(tl, L)
    d = pl.BlockSpec((None, tl, cw), lambda b, i: (b, i, 0))
    f = pl.BlockSpec((2, tl, cw), lambda b, i: (0, i, 0))
    shp = jax.ShapeDtypeStruct((nb, L, cw), BF16)
    return pl.pallas_call(
        functools.partial(_spec_mul_kernel, n2=2 * L), grid=(nb, L // tl),
        in_specs=[d, d, f, f], out_specs=[d, d], out_shape=[shp, shp],
        compiler_params=_cparams(("arbitrary", "arbitrary")),
        name="spec_mul",
    )(xr, xs, kp, kq)


def _dft_inv_kernel(c_ref, s_ref, yr_ref, yi_ref, x0_ref, zz_ref, bd_ref, o_ref):
    y = _dot(c_ref[...], yr_ref[...]) + _dot(s_ref[...], yi_ref[...])
    o_ref[...] = (x0_ref[...] * (y + zz_ref[...] * bd_ref[...])).astype(o_ref.dtype)


def dft_inv(ctab, s2tab, yr, yi, x0, zz, bias_d, tm=512):
    nb, L, cw = yr.shape
    tm = min(tm, L)
    nti = L // tm
    tab = pl.BlockSpec((tm, L), lambda i, b: (i, 0))
    full = pl.BlockSpec((None, L, cw), lambda i, b: (b, 0, 0))
    tile = pl.BlockSpec((None, tm, cw), lambda i, b: (b, i, 0))
    return pl.pallas_call(
        _dft_inv_kernel, grid=(nti, nb),
        in_specs=[tab, tab, full, full, tile, tile, pl.BlockSpec((1, cw), lambda i, b: (0, 0))],
        out_specs=pl.BlockSpec((tm, cw), lambda i, b: (b * nti + i, 0)),
        out_shape=jax.ShapeDtypeStruct((nb * L, cw), BF16),
        compiler_params=_cparams(("arbitrary", "arbitrary")),
        name="dft_inv",
    )(ctab, s2tab, yr, yi, x0, zz, bias_d.reshape(1, cw))


def _norm_router_kernel(s_ref, x_ref, g_ref, sc_ref, sh_ref, w_ref, b_ref, h_ref, o_ref):
    h = _rms(x_ref[...]) * g_ref[...]
    h = h * (1.0 + sc_ref[...]) + sh_ref[...]
    h_ref[...] = h
    logits = _dot_hi(h, w_ref[...]) + b_ref[...]
    li = lax.broadcasted_iota(jnp.int32, logits.shape, 1)
    big = jnp.int32(LANES)
    ninf = -jnp.inf

    def rmax(x):
        return jnp.max(x, axis=-1, keepdims=True)

    def first_at(x, m):
        return jnp.min(jnp.where(x == m, li, big), axis=-1, keepdims=True)

    gmask = li < N_GROUPS
    gl = jnp.where(gmask, logits, ninf)
    gmax = rmax(gl)
    g_sel = first_at(gl, gmax)
    g_prob = 1.0 / jnp.sum(jnp.where(gmask, jnp.exp(gl - gmax), 0.0), axis=-1, keepdims=True)
    e = li - N_GROUPS
    emask = (e >= 0) & (e < N_EXPERTS) & ((e >> 2) == g_sel)
    el = jnp.where(emask, logits, ninf)
    emax = rmax(el)
    esum = jnp.sum(jnp.where(emask, jnp.exp(el - emax), 0.0), axis=-1, keepdims=True)
    l1 = first_at(el, emax)
    el2 = jnp.where(li == l1, ninf, el)
    emax2 = rmax(el2)
    l2 = first_at(el2, emax2)
    p1 = 1.0 / esum
    p2 = jnp.exp(emax2 - emax) / esum
    tot = p1 + p2
    w1 = p1 / tot * g_prob
    w2 = p2 / tot * g_prob
    o_ref[...] = jnp.where(li == 0, (l1 - N_GROUPS).astype(F32),
                           jnp.where(li == 1, (l2 - N_GROUPS).astype(F32),
                                     jnp.where(li == 2, w1, jnp.where(li == 3, w2, 0.0))))


def norm_router(x1, g, mods, seg_ids, tm, wg, bg, we, be):
    nt, d = x1.shape
    wr = _pad2(jnp.concatenate([wg, we], axis=1), d, LANES)
    br = _pad2(jnp.concatenate([bg, be])[None], 1, LANES)
    gs = pltpu.PrefetchScalarGridSpec(
        num_scalar_prefetch=1, grid=(nt // tm,),
        in_specs=[pl.BlockSpec((tm, d), lambda i, s: (i, 0)),
                  pl.BlockSpec((1, d), lambda i, s: (0, 0)),
                  pl.BlockSpec((None, 1, d), lambda i, s: (s[i], 0, 4)),
                  pl.BlockSpec((None, 1, d), lambda i, s: (s[i], 0, 3)),
                  pl.BlockSpec((d, LANES), lambda i, s: (0, 0)),
                  pl.BlockSpec((1, LANES), lambda i, s: (0, 0))],
        out_specs=[pl.BlockSpec((tm, d), lambda i, s: (i, 0)),
                   pl.BlockSpec((tm, LANES), lambda i, s: (i, 0))])
    return pl.pallas_call(
        _norm_router_kernel, grid_spec=gs,
        out_shape=[jax.ShapeDtypeStruct((nt, d), F32), jax.ShapeDtypeStruct((nt, LANES), F32)],
        compiler_params=_cparams(("arbitrary",)),
        name="norm_router",
    )(seg_ids, x1, g.reshape(1, d), mods, mods, wr, br)


def moe_plan(route, tm):
    nt = route.shape[0]
    npair = nt * TOP_K
    ns = npair + N_EXPERTS * tm
    n_tiles = ns // tm
    e = route[:, :TOP_K].astype(jnp.int32).reshape(npair)
    onehot = (e[:, None] == jnp.arange(N_EXPERTS, dtype=jnp.int32)[None, :]).astype(jnp.int32)
    csum = jnp.cumsum(onehot, axis=0)
    counts = csum[-1]
    rank = jnp.sum((csum - onehot) * onehot, axis=1)
    padded = ((counts + tm - 1) // tm) * tm
    ends = jnp.cumsum(padded)
    starts = ends - padded
    slot = (jnp.sum(onehot * starts[None, :], axis=1) + rank).astype(jnp.int32)
    tile_start = jnp.arange(n_tiles, dtype=jnp.int32) * tm
    tile_e = jnp.sum((tile_start[:, None] >= ends[None, :]).astype(jnp.int32), axis=1)
    tile_valid = (tile_start < ends[-1]).astype(jnp.int32)
    last_e = jnp.max(jnp.where(counts > 0, jnp.arange(N_EXPERTS, dtype=jnp.int32), 0))
    tile_e = jnp.where(tile_valid > 0, tile_e, last_e).astype(jnp.int32)
    prev = jnp.concatenate([jnp.full((1,), -1, jnp.int32), tile_e[:-1]])
    tile_new = (tile_e != prev).astype(jnp.int32)
    meta = jnp.stack([tile_e, tile_valid, tile_new], axis=0)
    return meta, slot


ROW_UNROLL = 8


def _dispatch_kernel(slot_ref, h_ref, xs_in, xs_hbm, sem):
    del xs_in
    te = h_ref.shape[0]

    def row_copy(r, k):
        return pltpu.make_async_copy(h_ref.at[pl.ds(r, 1)], xs_hbm.at[pl.ds(slot_ref[0, TOP_K * r + k], 1)], sem)

    def start(r, c):
        for k in range(TOP_K):
            row_copy(r, k).start()
        return c

    lax.fori_loop(0, te, start, 0, unroll=ROW_UNROLL)

    def wait(r, c):
        for k in range(TOP_K):
            row_copy(r, k).wait()
        return c

    lax.fori_loop(0, te, wait, 0, unroll=ROW_UNROLL)


def moe_dispatch(h, slot, ns, te):
    nt, d = h.shape
    n = nt // te
    xs0 = jnp.zeros((ns, d), F32)
    return pl.pallas_call(
        _dispatch_kernel, grid=(n,),
        in_specs=[pl.BlockSpec((None, 1, TOP_K * te), lambda i: (i, 0, 0), memory_space=pltpu.SMEM),
                  pl.BlockSpec((te, d), lambda i: (i, 0)),
                  pl.BlockSpec(memory_space=pl.ANY)],
        out_specs=pl.BlockSpec(memory_space=pl.ANY),
        out_shape=jax.ShapeDtypeStruct((ns, d), F32),
        scratch_shapes=[pltpu.SemaphoreType.DMA],
        input_output_aliases={2: 0},
        compiler_params=_cparams(("arbitrary",)),
        name="moe_dispatch",
    )(slot.reshape(n, 1, TOP_K * te), h, xs0)


def _moe_kernel(meta_ref, x_ref, wg_ref, wu_ref, wd_ref, y_ref, wgb, wub, wdb):
    t = pl.program_id(0)

    @pl.when(meta_ref[1, t] > 0)
    def _():
        @pl.when(meta_ref[2, t] > 0)
        def _():
            wgb[...] = wg_ref[...].astype(BF16)
            wub[...] = wu_ref[...].astype(BF16)
            wdb[...] = wd_ref[...].astype(BF16)

        x = x_ref[...].astype(BF16)
        a = _dot(x, wgb[...])
        u = _dot(x, wub[...])
        hid = a * (1.0 / (1.0 + jnp.exp(-a))) * u
        y_ref[...] = _dot(hid.astype(BF16), wdb[...])

    @pl.when(meta_ref[1, t] == 0)
    def _():
        y_ref[...] = jnp.zeros_like(y_ref)


def moe_experts(xs, meta, w_gate, w_up, w_down, layer, tm):
    ns, d = xs.shape
    de = w_gate.shape[3]
    gs = pltpu.PrefetchScalarGridSpec(
        num_scalar_prefetch=1, grid=(ns // tm,),
        in_specs=[pl.BlockSpec((tm, d), lambda t, m: (t, 0)),
                  pl.BlockSpec((None, None, d, de), lambda t, m: (layer, m[0, t], 0, 0)),
                  pl.BlockSpec((None, None, d, de), lambda t, m: (layer, m[0, t], 0, 0)),
                  pl.BlockSpec((None, None, de, d), lambda t, m: (layer, m[0, t], 0, 0))],
        out_specs=pl.BlockSpec((tm, d), lambda t, m: (t, 0)),
        scratch_shapes=[pltpu.VMEM((d, de), BF16), pltpu.VMEM((d, de), BF16), pltpu.VMEM((de, d), BF16)])
    return pl.pallas_call(
        _moe_kernel, grid_spec=gs,
        out_shape=jax.ShapeDtypeStruct((ns, d), F32),
        compiler_params=_cparams(("arbitrary",)),
        name="moe_experts",
    )(meta, xs, w_gate, w_up, w_down)


def _moe_combine_kernel(s_ref, slot_ref, nslot_ref, x_ref, rt_ref, g2_ref, g_ref, sc_ref, sh_ref, ys_hbm,
                        x2_ref, h_ref, ybuf, sem, *, modulate):
    i = pl.program_id(0)
    n = pl.num_programs(0)
    te = x_ref.shape[0]

    def row_copy(idx_ref, buf, r, k):
        return pltpu.make_async_copy(ys_hbm.at[pl.ds(idx_ref[0, TOP_K * r + k], 1)],
                                     ybuf.at[buf, k, pl.ds(r, 1)], sem.at[buf])

    def start_tile(idx_ref, buf):
        def body(r, c):
            for k in range(TOP_K):
                row_copy(idx_ref, buf, r, k).start()
            return c
        lax.fori_loop(0, te, body, 0, unroll=ROW_UNROLL)

    cur = i % 2

    @pl.when(i == 0)
    def _():
        start_tile(slot_ref, 0)

    @pl.when(i + 1 < n)
    def _():
        start_tile(nslot_ref, 1 - cur)

    def wait_body(r, c):
        for k in range(TOP_K):
            row_copy(slot_ref, cur, r, k).wait()
        return c

    lax.fori_loop(0, te, wait_body, 0, unroll=ROW_UNROLL)
    w1 = rt_ref[:, TOP_K:TOP_K + 1]
    w2 = rt_ref[:, TOP_K + 1:TOP_K + 2]
    x2 = x_ref[...] + g2_ref[...] * (w1 * ybuf[cur, 0] + w2 * ybuf[cur, 1])
    x2_ref[...] = x2
    h = _rms(x2) * g_ref[...]
    if modulate:
        h = h * (1.0 + sc_ref[...]) + sh_ref[...]
    h_ref[...] = h.astype(h_ref.dtype)


def moe_combine(x1, ys, slot, route, g, mods_prev, mods_next, seg_ids, te, modulate, h_dtype):
    nt, d = x1.shape
    n = nt // te
    slot3 = slot.reshape(n, 1, TOP_K * te)
    gs = pltpu.PrefetchScalarGridSpec(
        num_scalar_prefetch=1, grid=(n,),
        in_specs=[pl.BlockSpec((None, 1, TOP_K * te), lambda i, s: (i, 0, 0), memory_space=pltpu.SMEM),
                  pl.BlockSpec((None, 1, TOP_K * te), lambda i, s: (jnp.minimum(i + 1, n - 1), 0, 0),
                               memory_space=pltpu.SMEM),
                  pl.BlockSpec((te, d), lambda i, s: (i, 0)),
                  pl.BlockSpec((te, LANES), lambda i, s: (i, 0)),
                  pl.BlockSpec((None, 1, d), lambda i, s: (s[i], 0, 5)),
                  pl.BlockSpec((1, d), lambda i, s: (0, 0)),
                  pl.BlockSpec((None, 1, d), lambda i, s: (s[i], 0, 1)),
                  pl.BlockSpec((None, 1, d), lambda i, s: (s[i], 0, 0)),
                  pl.BlockSpec(memory_space=pl.ANY)],
        out_specs=[pl.BlockSpec((te, d), lambda i, s: (i, 0)),
                   pl.BlockSpec((te, d), lambda i, s: (i, 0))],
        scratch_shapes=[pltpu.VMEM((2, TOP_K, te, d), F32), pltpu.SemaphoreType.DMA((2,))])
    return pl.pallas_call(
        functools.partial(_moe_combine_kernel, modulate=modulate), grid_spec=gs,
        out_shape=[jax.ShapeDtypeStruct((nt, d), F32), jax.ShapeDtypeStruct((nt, d), h_dtype)],
        compiler_params=_cparams(("arbitrary",)),
        name="moe_combine",
    )(seg_ids, slot3, slot3, x1, route, mods_prev, g.reshape(1, d), mods_next, mods_next, ys)


def kernel(x_prompt, x_sample, c, cache_na_k, cache_na_v, cache_gqa_k, cache_gqa_v, c_ctx, norm1_g, norm2_g, final_norm_g, ada_w, ada_b, w_in, w_out, na_rpb, q_norm_g, k_norm_g, hy_conv_w, hy_conv_b, hy_w1, hy_b1, hy_w2, hy_b2, hy_w3, hy_b3, hy_freq, hy_w_out, hy_bias, router_group_w, router_group_b, router_expert_w, router_expert_b, moe_w_gate, moe_w_up, moe_w_down):
    bc, lc, d = x_prompt.shape
    bl, ll, _ = x_sample.shape
    depth = w_in.shape[0]
    past = cache_na_k.shape[2]
    dh = HEAD_DIM
    na_w, gq_w = d // 4, d // 2
    hy_w = d - na_w - gq_w
    na_heads = na_w // dh
    nc = bc * lc
    nt = nc + bl * ll
    tm = 512
    moe_tm = 256
    assert nc % tm == 0 and ll % tm == 0 and past == lc

    te = 256

    def seg_of_tiles(t):
        return jnp.asarray(np.concatenate(
            [np.zeros(nc // t), 1 + np.repeat(np.arange(bl), ll // t)]).astype(np.int32))

    seg_ids = seg_of_tiles(tm)
    seg_e = seg_of_tiles(te)
    nseg = 1 + bl
    cond = jnp.zeros((8, d), F32).at[0].set(c_ctx).at[1:1 + bl].set(c)
    mods_all = modulation_all(cond, ada_w, ada_b)

    ropes = rope_tables(ll)
    tabs_c = dft_tables(lc)
    tabs_l = dft_tables(ll)
    ck_na = cache_na_k.reshape(bl, depth, past, na_w)
    cv_na = cache_na_v.reshape(bl, depth, past, na_w)
    ck_gq = cache_gqa_k.reshape(bl, depth, past, GQA_KV_HEADS * dh)
    cv_gq = cache_gqa_v.reshape(bl, depth, past, GQA_KV_HEADS * dh)

    x = jnp.concatenate([x_prompt.reshape(nc, d), x_sample.reshape(bl * ll, d)], axis=0)
    mods = mods_all[0, :nseg].reshape(nseg, 1, 6 * d)
    h = norm_mod(x, norm1_g[0], mods, 1, 0, seg_e, te, BF16)
    st_na_k, st_na_v, st_gq_k, st_gq_v = [], [], [], []
    hy_col0 = 3 * na_w + gq_w + 2 * GQA_KV_HEADS * dh
    for l in range(depth):
        z = in_proj(h, w_in, l)
        zc = z[:nc]
        st_na_k.append(zc[:, na_w:2 * na_w].reshape(bc, lc, na_heads, dh))
        st_na_v.append(zc[:, 2 * na_w:3 * na_w].reshape(bc, lc, na_heads, dh))
        st_gq_v.append(zc[:, 3 * na_w + gq_w + GQA_KV_HEADS * dh:hy_col0].reshape(bc, lc, GQA_KV_HEADS, dh))

        ctx_a, ctx_b, bk = ctx_attn(z, q_norm_g[l], k_norm_g[l], bc, lc, na_w, gq_w)
        st_gq_k.append(bk.reshape(bc, lc, GQA_KV_HEADS, dh))

        bias = na_bias(na_rpb[l])
        lat_a = lat_na(z, ck_na, cv_na, l, bias, nc, bl, ll, na_w)
        q, k, v = lat_prep(z, ropes, q_norm_g[l], k_norm_g[l], nc, bl, ll, gq_w)
        lat_b = lat_gqa(q, k, v, ck_gq, cv_gq, l)

        mix_c = []
        for (row0, nb, L, tabs) in ((0, bc, lc, tabs_c), (nc, bl, ll, tabs_l)):
            ctab, s1tab, s2tab = tabs
            x0, zz, zb = hy_pre(z, hy_conv_w[l], hy_conv_b[l], row0, nb, L, hy_col0, hy_w)
            filt = hy_filter(L, hy_w1[l], hy_b1[l], hy_w2[l], hy_b2[l], hy_w3[l], hy_b3[l],
                             hy_freq[l], hy_w_out[l])
            kp, kq = dft_fwd(ctab, s1tab, filt)
            xr, xs = dft_fwd(ctab, s1tab, zb)
            yr, yi = spec_mul(xr, xs, kp, kq)
            mix_c.append(dft_inv(ctab, s2tab, yr, yi, x0, zz, hy_bias[l]))

        x1 = out_proj((ctx_a, ctx_b, mix_c[0]), (lat_a, lat_b, mix_c[1]), w_out, l, x, mods, seg_ids)
        h2, route = norm_router(x1, norm2_g[l], mods, seg_e, te, router_group_w[l], router_group_b[l],
                                router_expert_w[l], router_expert_b[l])
        meta, slot = moe_plan(route, moe_tm)
        xs = moe_dispatch(h2, slot, nt * TOP_K + N_EXPERTS * moe_tm, te)
        ys = moe_experts(xs, meta, moe_w_gate, moe_w_up, moe_w_down, l, moe_tm)
        last = l == depth - 1
        mods_next = mods if last else mods_all[l + 1, :nseg].reshape(nseg, 1, 6 * d)
        g_next = final_norm_g if last else norm1_g[l + 1]
        x, h = moe_combine(x1, ys, slot, route, g_next, mods, mods_next, seg_e, te, not last,
                           F32 if last else BF16)
        mods = mods_next

    y_prompt = h[:nc].reshape(bc, lc, d)
    y_sample = h[nc:].reshape(bl, ll, d)
    return (y_prompt, y_sample, jnp.stack(st_na_k, axis=1), jnp.stack(st_na_v, axis=1),
            jnp.stack(st_gq_k, axis=1), jnp.stack(st_gq_v, axis=1))
```

```python
import functools
import math

import numpy as np
import jax
import jax.numpy as jnp
from jax import lax
from jax.experimental import pallas as pl
from jax.experimental.pallas import tpu as pltpu

F32 = jnp.float32
BF16 = jnp.bfloat16
HIGHEST = lax.Precision.HIGHEST

GRID_W = 64
HEAD_DIM = 128
NA_WIN_R = 8
NA_WIN_C = 16
GQA_KV_HEADS = 2
ROPE_THETA = 10000.0
HY_EMB = 33
HY_HID = 64
HY_FAST_DECAY = 0.3
HY_SLOW_DECAY = 1.5
HY_TARGET = 0.01
N_GROUPS = 4
EXP_PER_GROUP = 4
N_EXPERTS = N_GROUPS * EXP_PER_GROUP
TOP_K = 2
EPS = 1e-6
NEG_INF = -1e30

LANES = 128
VMEM_LIMIT = 52 * 1024 * 1024


def _cparams(sem, vmem=VMEM_LIMIT):
    return pltpu.CompilerParams(dimension_semantics=sem, vmem_limit_bytes=vmem)


def _dot(a, b):
    return jnp.dot(a, b, preferred_element_type=F32)


def _dot_t(a, b):
    return lax.dot_general(a, b, (((1,), (1,)), ((), ())), preferred_element_type=F32)


def _dot_hi(a, b):
    return jnp.dot(a, b, preferred_element_type=F32, precision=HIGHEST)


def _mod_kernel(c_ref, w_ref, b_ref, o_ref):
    c = c_ref[...]
    s = c * (1.0 / (1.0 + jnp.exp(-c)))
    o_ref[...] = _dot_hi(s, w_ref[...]) + b_ref[...]


def modulation_all(cond, ada_w, ada_b, tn=1536):
    depth, d, n = ada_w.shape
    r = cond.shape[0]
    return pl.pallas_call(
        _mod_kernel,
        grid=(depth, n // tn),
        in_specs=[pl.BlockSpec((r, d), lambda l, j: (0, 0)),
                  pl.BlockSpec((None, d, tn), lambda l, j: (l, 0, j)),
                  pl.BlockSpec((None, 1, tn), lambda l, j: (l, 0, j))],
        out_specs=pl.BlockSpec((None, r, tn), lambda l, j: (l, 0, j)),
        out_shape=jax.ShapeDtypeStruct((depth, r, n), F32),
        compiler_params=_cparams(("arbitrary", "arbitrary")),
        name="modulation",
    )(cond, ada_w, ada_b.reshape(depth, 1, n))


def _rms(x):
    return x * lax.rsqrt(jnp.mean(x * x, axis=-1, keepdims=True) + EPS)


def _norm_mod_kernel(s_ref, x_ref, g_ref, sc_ref, sh_ref, o_ref):
    h = _rms(x_ref[...]) * g_ref[...]
    o_ref[...] = (h * (1.0 + sc_ref[...]) + sh_ref[...]).astype(o_ref.dtype)


def norm_mod(x, g, mods, sc_idx, sh_idx, seg_ids, tm, out_dtype):
    nt, d = x.shape
    gs = pltpu.PrefetchScalarGridSpec(
        num_scalar_prefetch=1, grid=(nt // tm,),
        in_specs=[pl.BlockSpec((tm, d), lambda i, s: (i, 0)),
                  pl.BlockSpec((1, d), lambda i, s: (0, 0)),
                  pl.BlockSpec((None, 1, d), lambda i, s: (s[i], 0, sc_idx)),
                  pl.BlockSpec((None, 1, d), lambda i, s: (s[i], 0, sh_idx))],
        out_specs=pl.BlockSpec((tm, d), lambda i, s: (i, 0)))
    return pl.pallas_call(
        _norm_mod_kernel, grid_spec=gs,
        out_shape=jax.ShapeDtypeStruct((nt, d), out_dtype),
        compiler_params=_cparams(("arbitrary",)),
        name="norm_mod",
    )(seg_ids, x, g.reshape(1, d), mods, mods)


def _mm_kernel(a_ref, w_ref, o_ref, wbf_ref):
    @pl.when(pl.program_id(1) == 0)
    def _():
        wbf_ref[...] = w_ref[...].astype(BF16)

    o_ref[...] = _dot(a_ref[...], wbf_ref[...])


def in_proj(h, w_in, layer, tm=512, tn=768):
    nt, d = h.shape
    n = w_in.shape[2]
    return pl.pallas_call(
        _mm_kernel,
        grid=(n // tn, nt // tm),
        in_specs=[pl.BlockSpec((tm, d), lambda j, i: (i, 0)),
                  pl.BlockSpec((None, d, tn), lambda j, i: (layer, 0, j))],
        out_specs=pl.BlockSpec((tm, tn), lambda j, i: (i, j)),
        out_shape=jax.ShapeDtypeStruct((nt, n), F32),
        scratch_shapes=[pltpu.VMEM((d, tn), BF16)],
        compiler_params=_cparams(("arbitrary", "arbitrary")),
        name="in_proj",
    )(h, w_in)


def _softmax_pv(s, v):
    m = jnp.max(s, axis=-1, keepdims=True)
    p = jnp.exp(s - m)
    l = jnp.sum(p, axis=-1, keepdims=True)
    return _dot(p.astype(BF16), v) / l


def _ctx_attn_kernel(za_ref, zb_ref, qg_ref, kg_ref, a_ref, b_ref, bk_ref, *, na_heads, gq_heads):
    dh = HEAD_DIM
    scale = dh ** -0.5
    L = za_ref.shape[0]
    na_w = na_heads * dh
    for h in range(na_heads):
        q = za_ref[:, h * dh:(h + 1) * dh].astype(BF16)
        k = za_ref[:, na_w + h * dh:na_w + (h + 1) * dh].astype(BF16)
        v = za_ref[:, 2 * na_w + h * dh:2 * na_w + (h + 1) * dh].astype(BF16)
        s = _dot_t(q, k) * scale
        a_ref[:, h * dh:(h + 1) * dh] = _softmax_pv(s, v).astype(a_ref.dtype)
    gq_w = gq_heads * dh
    group = gq_heads // GQA_KV_HEADS
    for kv in range(GQA_KV_HEADS):
        kn = _rms(zb_ref[:, gq_w + kv * dh:gq_w + (kv + 1) * dh]) * kg_ref[...]
        bk_ref[:, kv * dh:(kv + 1) * dh] = kn
        knb = kn.astype(BF16)
        v = zb_ref[:, gq_w + (GQA_KV_HEADS + kv) * dh:gq_w + (GQA_KV_HEADS + kv + 1) * dh].astype(BF16)
        for g in range(group):
            h = kv * group + g
            qn = (_rms(zb_ref[:, h * dh:(h + 1) * dh]) * qg_ref[...]).astype(BF16)
            s = _dot_t(qn, knb) * scale
            b_ref[:, h * dh:(h + 1) * dh] = _softmax_pv(s, v).astype(b_ref.dtype)


def ctx_attn(z, q_g, k_g, bc, lc, na_w, gq_w):
    dh = HEAD_DIM
    wa = 3 * na_w
    wb = gq_w + 2 * GQA_KV_HEADS * dh
    assert wa == wb
    kern = functools.partial(_ctx_attn_kernel, na_heads=na_w // dh, gq_heads=gq_w // dh)
    return pl.pallas_call(
        kern, grid=(bc,),
        in_specs=[pl.BlockSpec((lc, wa), lambda b: (b, 0)),
                  pl.BlockSpec((lc, wb), lambda b: (b, 1)),
                  pl.BlockSpec((1, dh), lambda b: (0, 0)),
                  pl.BlockSpec((1, dh), lambda b: (0, 0))],
        out_specs=[pl.BlockSpec((lc, na_w), lambda b: (b, 0)),
                   pl.BlockSpec((lc, gq_w), lambda b: (b, 0)),
                   pl.BlockSpec((lc, GQA_KV_HEADS * dh), lambda b: (b, 0))],
        out_shape=[jax.ShapeDtypeStruct((bc * lc, na_w), BF16),
                   jax.ShapeDtypeStruct((bc * lc, gq_w), BF16),
                   jax.ShapeDtypeStruct((bc * lc, GQA_KV_HEADS * dh), F32)],
        compiler_params=_cparams(("arbitrary",)),
        name="ctx_attn",
    )(z, z, q_g.reshape(1, dh), k_g.reshape(1, dh))


def _rope_kernel(cos_ref, sa_ref, sb_ref):
    n, dh = cos_ref.shape
    quarter = dh // 2
    t = lax.broadcasted_iota(jnp.int32, (n, dh), 0)
    lane = lax.broadcasted_iota(jnp.int32, (n, dh), 1)
    f = (lane & (quarter // 2 - 1)).astype(F32)
    inv = jnp.exp(-(2.0 * f / quarter) * math.log(ROPE_THETA))
    pos = jnp.where(lane < quarter, t >> int(math.log2(GRID_W)), t & (GRID_W - 1)).astype(F32)
    ang = pos * inv
    sin = jnp.sin(ang)
    first = (lane & (quarter - 1)) < (quarter // 2)
    cos_ref[...] = jnp.cos(ang)
    sa_ref[...] = jnp.where(first, -sin, 0.0)
    sb_ref[...] = jnp.where(first, 0.0, sin)


def rope_tables(n):
    shp = jax.ShapeDtypeStruct((n, HEAD_DIM), F32)
    return pl.pallas_call(_rope_kernel, out_shape=[shp, shp, shp], name="rope_tables")()


def _rope(x, cos, sa, sb):
    q4 = HEAD_DIM // 4
    return x * cos + pltpu.roll(x, HEAD_DIM - q4, 1) * sa + pltpu.roll(x, q4, 1) * sb


def _with_ones(v):
    return jnp.concatenate([v, jnp.ones_like(v)], axis=1)


def _lat_prep_kernel(z_ref, cos_ref, sa_ref, sb_ref, qg_ref, kg_ref, q_ref, k_ref, v_ref, *, gq_heads):
    dh = HEAD_DIM
    scale = dh ** -0.5
    cos, sa, sb = cos_ref[...], sa_ref[...], sb_ref[...]
    for h in range(gq_heads):
        qn = _rms(z_ref[:, h * dh:(h + 1) * dh]) * qg_ref[...]
        q_ref[h] = (_rope(qn, cos, sa, sb) * scale).astype(BF16)
    gq_w = gq_heads * dh
    for kv in range(GQA_KV_HEADS):
        kn = _rms(z_ref[:, gq_w + kv * dh:gq_w + (kv + 1) * dh]) * kg_ref[...]
        k_ref[kv] = _rope(kn, cos, sa, sb).astype(BF16)
        v = z_ref[:, gq_w + (GQA_KV_HEADS + kv) * dh:gq_w + (GQA_KV_HEADS + kv + 1) * dh].astype(BF16)
        v_ref[kv] = _with_ones(v)


def lat_prep(z, ropes, q_g, k_g, nc, bl, ll, gq_w, tl=512):
    dh = HEAD_DIM
    gq_heads = gq_w // dh
    wb = gq_w + 2 * GQA_KV_HEADS * dh
    nb = ll // tl
    off = nc // tl
    cos, sa, sb = ropes
    tab = pl.BlockSpec((tl, dh), lambda b, i: (i, 0))
    one = pl.BlockSpec((1, dh), lambda b, i: (0, 0))
    return pl.pallas_call(
        functools.partial(_lat_prep_kernel, gq_heads=gq_heads),
        grid=(bl, nb),
        in_specs=[pl.BlockSpec((tl, wb), lambda b, i: (off + b * nb + i, 1)), tab, tab, tab, one, one],
        out_specs=[pl.BlockSpec((None, gq_heads, tl, dh), lambda b, i: (b, 0, i, 0)),
                   pl.BlockSpec((None, GQA_KV_HEADS, tl, dh), lambda b, i: (b, 0, i, 0)),
                   pl.BlockSpec((None, GQA_KV_HEADS, tl, 2 * dh), lambda b, i: (b, 0, i, 0))],
        out_shape=[jax.ShapeDtypeStruct((bl, gq_heads, ll, dh), BF16),
                   jax.ShapeDtypeStruct((bl, GQA_KV_HEADS, ll, dh), BF16),
                   jax.ShapeDtypeStruct((bl, GQA_KV_HEADS, ll, 2 * dh), BF16)],
        compiler_params=_cparams(("arbitrary", "arbitrary")),
        name="lat_prep",
    )(z, cos, sa, sb, q_g.reshape(1, dh), k_g.reshape(1, dh))


def _lat_gqa_kernel(q_ref, k_ref, v_ref, ck_ref, cv_ref, o_ref, *, tk):
    group, tq, dh = q_ref.shape
    rows = group * tq
    q = q_ref[...].reshape(rows, dh)
    ll = k_ref.shape[0]

    def update(carry, kc, vc):
        m, acc = carry
        s = _dot_t(q, kc)
        m_new = jnp.maximum(m, jnp.max(s, axis=-1, keepdims=True))
        alpha = jnp.exp(m - m_new)
        p = jnp.exp((s - jnp.concatenate([m_new] * (s.shape[1] // dh), axis=1)).astype(BF16))
        acc = jnp.concatenate([alpha, alpha], axis=1) * acc + _dot(p, vc)
        return m_new, acc

    def body(c, carry):
        st = pl.multiple_of(c * tk, tk)
        return update(carry, k_ref[pl.ds(st, tk), :], v_ref[pl.ds(st, tk), :])

    init = (jnp.full((rows, dh), -jnp.inf, F32), jnp.zeros((rows, 2 * dh), F32))
    carry = lax.fori_loop(0, ll // tk, body, init)
    m, acc = update(carry, ck_ref[...].astype(BF16), _with_ones(cv_ref[...].astype(BF16)))
    out = acc[:, :dh] / acc[:, dh:]
    for g in range(group):
        o_ref[:, g * dh:(g + 1) * dh] = out[g * tq:(g + 1) * tq].astype(o_ref.dtype)


def lat_gqa(q, k, v, cache_k, cache_v, layer, tq=256, tk=512):
    bl, gq_heads, ll, dh = q.shape
    group = gq_heads // GQA_KV_HEADS
    past = cache_k.shape[2]
    nq = ll // tq
    tk = min(tk, ll)
    ck = pl.BlockSpec((None, None, past, dh), lambda b, h, i: (b, layer, 0, h))
    return pl.pallas_call(
        functools.partial(_lat_gqa_kernel, tk=tk),
        grid=(bl, GQA_KV_HEADS, nq),
        in_specs=[pl.BlockSpec((None, group, tq, dh), lambda b, h, i: (b, h, i, 0)),
                  pl.BlockSpec((None, None, ll, dh), lambda b, h, i: (b, h, 0, 0)),
                  pl.BlockSpec((None, None, ll, 2 * dh), lambda b, h, i: (b, h, 0, 0)),
                  ck, ck],
        out_specs=pl.BlockSpec((tq, group * dh), lambda b, h, i: (b * nq + i, h)),
        out_shape=jax.ShapeDtypeStruct((bl * ll, gq_heads * dh), BF16),
        compiler_params=_cparams(("arbitrary", "arbitrary", "arbitrary")),
        name="lat_gqa",
    )(q, k, v, cache_k, cache_v)


def _na_bias_kernel(rpb_ref, o_ref):
    h = pl.program_id(0)
    w = GRID_W
    qc = lax.broadcasted_iota(jnp.int32, (w, w), 0)
    kc = lax.broadcasted_iota(jnp.int32, (w, w), 1)
    ci = jnp.clip(kc - qc + NA_WIN_C - 1, 0, 2 * NA_WIN_C - 2)
    cs = jnp.clip(qc - NA_WIN_C // 2, 0, w - NA_WIN_C)
    col_ok = (kc >= cs) & (kc < cs + NA_WIN_C)
    nr, ncol = 2 * NA_WIN_R - 1, 2 * NA_WIN_C - 1
    tiles = []
    for dr in range(nr):
        t = jnp.zeros((w, w), F32)
        for c in range(ncol):
            t = jnp.where(ci == c, rpb_ref[h, dr * ncol + c], t)
        tiles.append(jnp.where(col_ok, t, NEG_INF))
    for d0 in range(NA_WIN_R):
        for j in range(NA_WIN_R):
            o_ref[d0, :, j * w:(j + 1) * w] = tiles[d0 + j]


def na_bias(rpb_l):
    heads = rpb_l.shape[0]
    flat = rpb_l.reshape(heads, -1)
    return pl.pallas_call(
        _na_bias_kernel, grid=(heads,),
        in_specs=[pl.BlockSpec(memory_space=pltpu.SMEM)],
        out_specs=pl.BlockSpec((None, NA_WIN_R, GRID_W, NA_WIN_R * GRID_W), lambda h: (h, 0, 0, 0)),
        out_shape=jax.ShapeDtypeStruct((heads, NA_WIN_R, GRID_W, NA_WIN_R * GRID_W), F32),
        compiler_params=_cparams(("arbitrary",)),
        name="na_bias",
    )(flat)


def _lat_na_kernel(q_ref, k_ref, v_ref, ck_ref, cv_ref, bias_ref, o_ref, kb_ref, vb_ref, *, rb, rows):
    w = GRID_W
    scale = HEAD_DIM ** -0.5
    i = pl.program_id(2)

    @pl.when(i == 0)
    def _():
        kb_ref[...] = k_ref[...].astype(BF16)
        vb_ref[...] = v_ref[...].astype(BF16)

    ck = ck_ref[...].astype(BF16)
    cv = cv_ref[...].astype(BF16)
    for j in range(rb):
        r = i * rb + j
        rs = jnp.clip(r - NA_WIN_R // 2, 0, rows - NA_WIN_R)
        d0 = rs - r + NA_WIN_R - 1
        st = pl.multiple_of(rs * w, w)
        q = q_ref[j * w:(j + 1) * w, :].astype(BF16)
        kw = kb_ref[pl.ds(st, NA_WIN_R * w), :]
        vw = vb_ref[pl.ds(st, NA_WIN_R * w), :]
        s1 = _dot_t(q, kw) * scale + bias_ref[d0]
        s2 = _dot_t(q, ck) * scale
        m = jnp.maximum(jnp.max(s1, axis=-1, keepdims=True), jnp.max(s2, axis=-1, keepdims=True))
        p1 = jnp.exp(s1 - m)
        p2 = jnp.exp(s2 - m)
        l = jnp.sum(p1, axis=-1, keepdims=True) + jnp.sum(p2, axis=-1, keepdims=True)
        o = (_dot(p1.astype(BF16), vw) + _dot(p2.astype(BF16), cv)) / l
        o_ref[j * w:(j + 1) * w, :] = o.astype(o_ref.dtype)


def lat_na(z, cache_k, cache_v, layer, bias, nc, bl, ll, na_w, rb=8):
    dh = HEAD_DIM
    heads = na_w // dh
    rows = ll // GRID_W
    assert rows >= NA_WIN_R and nc % ll == 0 and rows % rb == 0
    past = cache_k.shape[2]
    nb = rows // rb
    tq = rb * GRID_W
    off = nc // tq
    lat0 = nc // ll
    ck = pl.BlockSpec((None, None, past, dh), lambda b, h, i: (b, layer, 0, h))
    return pl.pallas_call(
        functools.partial(_lat_na_kernel, rb=rb, rows=rows),
        grid=(bl, heads, nb),
        in_specs=[pl.BlockSpec((tq, dh), lambda b, h, i: (off + b * nb + i, h)),
                  pl.BlockSpec((ll, dh), lambda b, h, i: (lat0 + b, heads + h)),
                  pl.BlockSpec((ll, dh), lambda b, h, i: (lat0 + b, 2 * heads + h)),
                  ck, ck,
                  pl.BlockSpec((None, NA_WIN_R, GRID_W, NA_WIN_R * GRID_W), lambda b, h, i: (h, 0, 0, 0))],
        out_specs=pl.BlockSpec((tq, dh), lambda b, h, i: (b * nb + i, h)),
        out_shape=jax.ShapeDtypeStruct((bl * ll, na_w), BF16),
        scratch_shapes=[pltpu.VMEM((ll, dh), BF16), pltpu.VMEM((ll, dh), BF16)],
        compiler_params=_cparams(("arbitrary", "arbitrary", "arbitrary")),
        name="lat_na",
    )(z, z, z, cache_k, cache_v, bias)


def _hy_pre_kernel(u0_ref, u1_ref, u2_ref, w0_ref, w1_ref, w2_ref, b0_ref, b1_ref, b2_ref,
                   x0_ref, zz_ref, zb_ref):
    L = u0_ref.shape[0]
    t = lax.broadcasted_iota(jnp.int32, u0_ref.shape, 0)

    def conv(u_ref, w_ref, b_ref):
        u = u_ref[...]
        prev = jnp.where(t == 0, 0.0, pltpu.roll(u, 1, 0))
        nxt = jnp.where(t == L - 1, 0.0, pltpu.roll(u, L - 1, 0))
        return prev * w_ref[0:1, :] + u * w_ref[1:2, :] + nxt * w_ref[2:3, :] + b_ref[...]

    x0_ref[...] = conv(u0_ref, w0_ref, b0_ref)
    zz = conv(u2_ref, w2_ref, b2_ref) * conv(u1_ref, w1_ref, b1_ref)
    zz_ref[...] = zz
    zb_ref[...] = zz.astype(BF16)


def hy_pre(z, conv_w, conv_b, row0, nb, L, col0, hy_w):
    cb = LANES
    nj = hy_w // cb
    c0 = col0 // cb
    r0 = row0 // L
    assert row0 % L == 0
    u = lambda part: pl.BlockSpec((L, cb), lambda b, j: (r0 + b, c0 + part * nj + j))
    wspec = lambda part: pl.BlockSpec((3, cb), lambda b, j: (0, part * nj + j))
    bspec = lambda part: pl.BlockSpec((1, cb), lambda b, j: (0, part * nj + j))
    o = pl.BlockSpec((None, L, cb), lambda b, j: (b, 0, j))
    return pl.pallas_call(
        _hy_pre_kernel, grid=(nb, nj),
        in_specs=[u(0), u(1), u(2), wspec(0), wspec(1), wspec(2), bspec(0), bspec(1), bspec(2)],
        out_specs=[o, o, o],
        out_shape=[jax.ShapeDtypeStruct((nb, L, hy_w), F32), jax.ShapeDtypeStruct((nb, L, hy_w), F32),
                   jax.ShapeDtypeStruct((nb, L, hy_w), BF16)],
        compiler_params=_cparams(("arbitrary", "arbitrary")),
        name="hy_pre",
    )(z, z, z, conv_w, conv_w, conv_w, conv_b.reshape(1, -1), conv_b.reshape(1, -1), conv_b.reshape(1, -1))


def _hy_filter_kernel(w1_ref, b1_ref, w2_ref, b2_ref, w3_ref, b3_ref, fq_ref, wo_ref, o_ref, *, hy_w):
    L = o_ref.shape[1]
    bands = (HY_EMB - 1) // 2
    ti = lax.broadcasted_iota(jnp.int32, (L, LANES), 0).astype(F32)
    lane = lax.broadcasted_iota(jnp.int32, (L, LANES), 1)
    t01 = ti / (L - 1)
    w = (2.0 * math.pi / L) * ti
    band = ((lane - 1) % bands).astype(F32)
    fr = 1e-4 + band * ((bands - 1 - 1e-4) / (bands - 1))
    ang = fr * w
    feat = jnp.where(lane == 0, t01,
                     jnp.where(lane <= bands, jnp.cos(ang),
                               jnp.where(lane <= 2 * bands, -jnp.sin(ang), 0.0)))
    fq = fq_ref[...]
    hdn = jnp.sin(fq * (_dot_hi(feat, w1_ref[...]) + b1_ref[...]))
    hdn = jnp.sin(fq * (_dot_hi(hdn, w2_ref[...]) + b2_ref[...]))
    hdn = jnp.sin(fq * (_dot_hi(hdn, w3_ref[...]) + b3_ref[...]))
    filt = _dot_hi(hdn, wo_ref[...])
    max_decay = math.log(HY_TARGET) / HY_FAST_DECAY
    min_decay = math.log(HY_TARGET) / HY_SLOW_DECAY
    ch = lax.broadcasted_iota(jnp.int32, (L, hy_w), 1).astype(F32)
    deltas = jnp.abs(min_decay + ch * ((max_decay - min_decay) / (hy_w - 1)))
    tt = lax.broadcasted_iota(jnp.int32, (L, hy_w), 0)
    decay = jnp.exp(-(tt.astype(F32) / (L - 1)) * deltas)
    o_ref[0] = (filt[:, :hy_w] * decay).astype(o_ref.dtype)
    o_ref[1] = jnp.where(tt == 0, 0.0, filt[:, hy_w:] * decay).astype(o_ref.dtype)


def _pad2(a, r, c):
    return jnp.pad(a, ((0, r - a.shape[0]), (0, c - a.shape[1])))


def hy_filter(L, w1, b1, w2, b2, w3, b3, freq, w_out):
    hy_w = w_out.shape[1] // 2
    p = LANES
    args = (_pad2(w1, p, p), _pad2(b1[None], 1, p), _pad2(w2, p, p), _pad2(b2[None], 1, p),
            _pad2(w3, p, p), _pad2(b3[None], 1, p), _pad2(freq[None], 1, p), _pad2(w_out, p, 2 * hy_w))
    return pl.pallas_call(
        functools.partial(_hy_filter_kernel, hy_w=hy_w),
        out_shape=jax.ShapeDtypeStruct((2, L, hy_w), BF16),
        compiler_params=_cparams(None),
        name="hy_filter",
    )(*args)


def _dft_tables_kernel(c_ref, s1_ref, s2_ref, *, L):
    tk = c_ref.shape[0]
    n2 = 2 * L
    theta = 2.0 * math.pi / n2
    k = pl.program_id(0) * tk + lax.broadcasted_iota(jnp.int32, (tk, LANES), 0)
    lane = lax.broadcasted_iota(jnp.int32, (tk, LANES), 1)
    a = ((k * lane) & (n2 - 1)).astype(F32) * theta
    b = ((k * LANES * lane) & (n2 - 1)).astype(F32) * theta
    ca, sa, cb, sb = jnp.cos(a), jnp.sin(a), jnp.cos(b), jnp.sin(b)
    sign_k = (1 - 2 * (k & 1)).astype(F32)
    for j in range(L // LANES):
        cbj = cb[:, j:j + 1]
        sbj = sb[:, j:j + 1]
        c = ca * cbj - sa * sbj
        s = sa * cbj + ca * sbj
        t = lane + j * LANES
        sign_t = (1 - 2 * (t & 1)).astype(F32)
        c_ref[:, j * LANES:(j + 1) * LANES] = c.astype(BF16)
        s1_ref[:, j * LANES:(j + 1) * LANES] = jnp.where(k == 0, sign_t, s).astype(BF16)
        s2_ref[:, j * LANES:(j + 1) * LANES] = jnp.where(t == 0, sign_k, s).astype(BF16)


def dft_tables(L, tk=256):
    tk = min(tk, L)
    shp = jax.ShapeDtypeStruct((L, L), BF16)
    spec = pl.BlockSpec((tk, L), lambda i: (i, 0))
    return pl.pallas_call(
        functools.partial(_dft_tables_kernel, L=L), grid=(L // tk,),
        out_specs=[spec, spec, spec], out_shape=[shp, shp, shp],
        compiler_params=_cparams(("arbitrary",)),
        name="dft_tables",
    )()


def _dft_fwd_kernel(c_ref, s_ref, z_ref, re_ref, im_ref):
    z = z_ref[...]
    re_ref[...] = _dot(c_ref[...], z)
    im_ref[...] = _dot(s_ref[...], z)


def dft_fwd(ctab, stab, zb, tm=512):
    nb, L, cw = zb.shape
    tm = min(tm, L)
    tab = pl.BlockSpec((tm, L), lambda i, b: (i, 0))
    o = pl.BlockSpec((None, tm, cw), lambda i, b: (b, i, 0))
    shp = jax.ShapeDtypeStruct((nb, L, cw), F32)
    return pl.pallas_call(
        _dft_fwd_kernel, grid=(L // tm, nb),
        in_specs=[tab, tab, pl.BlockSpec((None, L, cw), lambda i, b: (b, 0, 0))],
        out_specs=[o, o], out_shape=[shp, shp],
        compiler_params=_cparams(("arbitrary", "arbitrary")),
        name="dft_fwd",
    )(ctab, stab, zb)


def _spec_mul_kernel(xr_ref, xs_ref, kp_ref, kq_ref, yr_ref, yi_ref, *, n2):
    tl = xr_ref.shape[0]
    k = pl.program_id(1) * tl + lax.broadcasted_iota(jnp.int32, xr_ref.shape, 0)
    xr, xs = xr_ref[...], xs_ref[...]
    kre = kp_ref[0] + kp_ref[1]
    kim = kq_ref[1] - kq_ref[0]
    knyq = kq_ref[0] + kq_ref[1]
    first = k == 0
    yre = xr * kre + jnp.where(first, 0.0, xs * kim)
    yim = xr * kim - xs * kre
    yr_ref[...] = (jnp.where(first, 1.0 / n2, 2.0 / n2) * yre).astype(BF16)
    yi_ref[...] = jnp.where(first, xs * knyq * (1.0 / n2), (-2.0 / n2) * yim).astype(BF16)


def spec_mul(xr, xs, kp, kq, tl=512):
    nb, L, cw = xr.shape
    tl = min(tl, L)
    d = pl.BlockSpec((None, tl, cw), lambda b, i: (b, i, 0))
    f = pl.BlockSpec((2, tl, cw), lambda b, i: (0, i, 0))
    shp = jax.ShapeDtypeStruct((nb, L, cw), BF16)
    return pl.pallas_call(
        functools.partial(_spec_mul_kernel, n2=2 * L), grid=(nb, L // tl),
        in_specs=[d, d, f, f], out_specs=[d, d], out_shape=[shp, shp],
        compiler_params=_cparams(("arbitrary", "arbitrary")),
        name="spec_mul",
    )(xr, xs, kp, kq)


def _dft_inv_kernel(c_ref, s_ref, yr_ref, yi_ref, x0_ref, zz_ref, bd_ref, o_ref):
    y = _dot(c_ref[...], yr_ref[...]) + _dot(s_ref[...], yi_ref[...])
    o_ref[...] = (x0_ref[...] * (y + zz_ref[...] * bd_ref[...])).astype(o_ref.dtype)


def dft_inv(ctab, s2tab, yr, yi, x0, zz, bias_d, tm=512):
    nb, L, cw = yr.shape
    tm = min(tm, L)
    nti = L // tm
    tab = pl.BlockSpec((tm, L), lambda i, b: (i, 0))
    full = pl.BlockSpec((None, L, cw), lambda i, b: (b, 0, 0))
    tile = pl.BlockSpec((None, tm, cw), lambda i, b: (b, i, 0))
    return pl.pallas_call(
        _dft_inv_kernel, grid=(nti, nb),
        in_specs=[tab, tab, full, full, tile, tile, pl.BlockSpec((1, cw), lambda i, b: (0, 0))],
        out_specs=pl.BlockSpec((tm, cw), lambda i, b: (b * nti + i, 0)),
        out_shape=jax.ShapeDtypeStruct((nb * L, cw), BF16),
        compiler_params=_cparams(("arbitrary", "arbitrary")),
        name="dft_inv",
    )(ctab, s2tab, yr, yi, x0, zz, bias_d.reshape(1, cw))


def _split_bf16(a):
    hi = a.astype(BF16)
    return hi, (a - hi.astype(F32)).astype(BF16)


def _dot_3pass(a, b):
    a_hi, a_lo = _split_bf16(a)
    b_hi, b_lo = _split_bf16(b)
    return _dot(a_hi, b_hi) + (_dot(a_lo, b_hi) + _dot(a_hi, b_lo))


def _route(logits):
    li = lax.broadcasted_iota(jnp.int32, logits.shape, 1)
    big = jnp.int32(LANES)
    ninf = -jnp.inf

    def rmax(x):
        return jnp.max(x, axis=-1, keepdims=True)

    def first_at(x, m):
        return jnp.min(jnp.where(x == m, li, big), axis=-1, keepdims=True)

    gmask = li < N_GROUPS
    gl = jnp.where(gmask, logits, ninf)
    gmax = rmax(gl)
    g_sel = first_at(gl, gmax)
    g_prob = 1.0 / jnp.sum(jnp.where(gmask, jnp.exp(gl - gmax), 0.0), axis=-1, keepdims=True)
    e = li - N_GROUPS
    emask = (e >= 0) & (e < N_EXPERTS) & ((e >> 2) == g_sel)
    el = jnp.where(emask, logits, ninf)
    emax = rmax(el)
    esum = jnp.sum(jnp.where(emask, jnp.exp(el - emax), 0.0), axis=-1, keepdims=True)
    l1 = first_at(el, emax)
    el2 = jnp.where(li == l1, ninf, el)
    emax2 = rmax(el2)
    l2 = first_at(el2, emax2)
    p1 = 1.0 / esum
    p2 = jnp.exp(emax2 - emax) / esum
    tot = p1 + p2
    w1 = p1 / tot * g_prob
    w2 = p2 / tot * g_prob
    return jnp.where(li == 0, (l1 - N_GROUPS).astype(F32),
                     jnp.where(li == 1, (l2 - N_GROUPS).astype(F32),
                               jnp.where(li == 2, w1, jnp.where(li == 3, w2, 0.0))))


def _out_router_kernel(s_ref, ca_ref, cb_ref, cc_ref, la_ref, lb_ref, lc_ref, w_ref, x_ref, g1_ref,
                       g_ref, sc_ref, sh_ref, wr_ref, br_ref, x1_ref, h_ref, rt_ref, *, nci):
    i = pl.program_id(0)

    def project(a_ref, b_ref, c_ref):
        wa = a_ref.shape[1]
        wb = b_ref.shape[1]
        acc = _dot(a_ref[...], w_ref[0:wa, :])
        acc += _dot(b_ref[...], w_ref[wa:wa + wb, :])
        acc += _dot(c_ref[...], w_ref[wa + wb:, :])
        x1_ref[...] = x_ref[...] + g1_ref[...] * acc

    @pl.when(i < nci)
    def _():
        project(ca_ref, cb_ref, cc_ref)

    @pl.when(i >= nci)
    def _():
        project(la_ref, lb_ref, lc_ref)

    h = _rms(x1_ref[...]) * g_ref[...]
    h = h * (1.0 + sc_ref[...]) + sh_ref[...]
    h_ref[...] = h
    rt_ref[...] = _route(_dot_3pass(h, wr_ref[...]) + br_ref[...])


def out_proj_router(mix_ctx, mix_lat, w_out_bf, x, g, mods, seg_ids, tm, wg, bg, we, be):
    nt, d = x.shape
    nci = mix_ctx[0].shape[0] // tm
    wr = _pad2(jnp.concatenate([wg, we], axis=1), d, LANES)
    br = _pad2(jnp.concatenate([bg, be])[None], 1, LANES)
    cspec = lambda a: pl.BlockSpec((tm, a.shape[1]), lambda i, s: (jnp.minimum(i, nci - 1), 0))
    lspec = lambda a: pl.BlockSpec((tm, a.shape[1]), lambda i, s: (jnp.maximum(i - nci, 0), 0))
    row = pl.BlockSpec((tm, d), lambda i, s: (i, 0))
    mod = lambda p: pl.BlockSpec((None, 1, d), lambda i, s: (s[i], 0, p))
    gs = pltpu.PrefetchScalarGridSpec(
        num_scalar_prefetch=1, grid=(nt // tm,),
        in_specs=[cspec(mix_ctx[0]), cspec(mix_ctx[1]), cspec(mix_ctx[2]),
                  lspec(mix_lat[0]), lspec(mix_lat[1]), lspec(mix_lat[2]),
                  pl.BlockSpec((d, d), lambda i, s: (0, 0)),
                  row, mod(2),
                  pl.BlockSpec((1, d), lambda i, s: (0, 0)), mod(4), mod(3),
                  pl.BlockSpec((d, LANES), lambda i, s: (0, 0)),
                  pl.BlockSpec((1, LANES), lambda i, s: (0, 0))],
        out_specs=[row, row, pl.BlockSpec((tm, LANES), lambda i, s: (i, 0))])
    return pl.pallas_call(
        functools.partial(_out_router_kernel, nci=nci), grid_spec=gs,
        out_shape=[jax.ShapeDtypeStruct((nt, d), F32), jax.ShapeDtypeStruct((nt, d), F32),
                   jax.ShapeDtypeStruct((nt, LANES), F32)],
        compiler_params=_cparams(("arbitrary",)),
        name="out_proj_router",
    )(seg_ids, *mix_ctx, *mix_lat, w_out_bf, x, mods, g.reshape(1, d), mods, mods, wr, br)


def moe_plan(route, tm):
    nt = route.shape[0]
    npair = nt * TOP_K
    ns = npair + N_EXPERTS * tm
    n_tiles = ns // tm
    e = route[:, :TOP_K].astype(jnp.int32).reshape(npair)
    onehot = (e[:, None] == jnp.arange(N_EXPERTS, dtype=jnp.int32)[None, :]).astype(jnp.int32)
    csum = jnp.cumsum(onehot, axis=0)
    counts = csum[-1]
    rank = jnp.sum((csum - onehot) * onehot, axis=1)
    padded = ((counts + tm - 1) // tm) * tm
    ends = jnp.cumsum(padded)
    starts = ends - padded
    slot = (jnp.sum(onehot * starts[None, :], axis=1) + rank).astype(jnp.int32)
    tile_start = jnp.arange(n_tiles, dtype=jnp.int32) * tm
    tile_e = jnp.sum((tile_start[:, None] >= ends[None, :]).astype(jnp.int32), axis=1)
    tile_valid = (tile_start < ends[-1]).astype(jnp.int32)
    last_e = jnp.max(jnp.where(counts > 0, jnp.arange(N_EXPERTS, dtype=jnp.int32), 0))
    tile_e = jnp.where(tile_valid > 0, tile_e, last_e).astype(jnp.int32)
    prev = jnp.concatenate([jnp.full((1,), -1, jnp.int32), tile_e[:-1]])
    tile_new = (tile_e != prev).astype(jnp.int32)
    meta = jnp.stack([tile_e, tile_valid, tile_new], axis=0)
    return meta, slot


ROW_UNROLL = 8


def _dispatch_kernel(slot_ref, h_ref, xs_in, xs_hbm, sem):
    del xs_in
    te = h_ref.shape[0]

    def row_copy(r, k):
        return pltpu.make_async_copy(h_ref.at[pl.ds(r, 1)], xs_hbm.at[pl.ds(slot_ref[0, TOP_K * r + k], 1)], sem)

    def start(r, c):
        for k in range(TOP_K):
            row_copy(r, k).start()
        return c

    lax.fori_loop(0, te, start, 0, unroll=ROW_UNROLL)

    def wait(r, c):
        for k in range(TOP_K):
            row_copy(r, k).wait()
        return c

    lax.fori_loop(0, te, wait, 0, unroll=ROW_UNROLL)


def moe_dispatch(h, slot, xs_prev, te):
    nt, d = h.shape
    n = nt // te
    ns = xs_prev.shape[0]
    return pl.pallas_call(
        _dispatch_kernel, grid=(n,),
        in_specs=[pl.BlockSpec((None, 1, TOP_K * te), lambda i: (i, 0, 0), memory_space=pltpu.SMEM),
                  pl.BlockSpec((te, d), lambda i: (i, 0)),
                  pl.BlockSpec(memory_space=pl.ANY)],
        out_specs=pl.BlockSpec(memory_space=pl.ANY),
        out_shape=jax.ShapeDtypeStruct((ns, d), F32),
        scratch_shapes=[pltpu.SemaphoreType.DMA],
        input_output_aliases={2: 0},
        compiler_params=_cparams(("arbitrary",)),
        name="moe_dispatch",
    )(slot.reshape(n, 1, TOP_K * te), h, xs_prev)


def _moe_kernel(meta_ref, x_ref, wg_ref, wu_ref, wd_ref, y_ref, wgb, wub, wdb):
    t = pl.program_id(0)

    @pl.when(meta_ref[1, t] > 0)
    def _():
        @pl.when(meta_ref[2, t] > 0)
        def _():
            wgb[...] = wg_ref[...].astype(BF16)
            wub[...] = wu_ref[...].astype(BF16)
            wdb[...] = wd_ref[...].astype(BF16)

        x = x_ref[...].astype(BF16)
        a = _dot(x, wgb[...])
        u = _dot(x, wub[...])
        hid = a * (1.0 / (1.0 + jnp.exp(-a))) * u
        y_ref[...] = _dot(hid.astype(BF16), wdb[...])

    @pl.when(meta_ref[1, t] == 0)
    def _():
        y_ref[...] = jnp.zeros_like(y_ref)


def moe_experts(xs, meta, w_gate, w_up, w_down, layer, tm):
    ns, d = xs.shape
    de = w_gate.shape[3]
    gs = pltpu.PrefetchScalarGridSpec(
        num_scalar_prefetch=1, grid=(ns // tm,),
        in_specs=[pl.BlockSpec((tm, d), lambda t, m: (t, 0)),
                  pl.BlockSpec((None, None, d, de), lambda t, m: (layer, m[0, t], 0, 0)),
                  pl.BlockSpec((None, None, d, de), lambda t, m: (layer, m[0, t], 0, 0)),
                  pl.BlockSpec((None, None, de, d), lambda t, m: (layer, m[0, t], 0, 0))],
        out_specs=pl.BlockSpec((tm, d), lambda t, m: (t, 0)),
        scratch_shapes=[pltpu.VMEM((d, de), BF16), pltpu.VMEM((d, de), BF16), pltpu.VMEM((de, d), BF16)])
    return pl.pallas_call(
        _moe_kernel, grid_spec=gs,
        out_shape=jax.ShapeDtypeStruct((ns, d), F32),
        compiler_params=_cparams(("arbitrary",)),
        name="moe_experts",
    )(meta, xs, w_gate, w_up, w_down)


def _moe_combine_kernel(s_ref, slot_ref, nslot_ref, x_ref, rt_ref, g2_ref, g_ref, sc_ref, sh_ref, ys_hbm,
                        x2_ref, h_ref, ybuf, sem, *, modulate):
    i = pl.program_id(0)
    n = pl.num_programs(0)
    te = x_ref.shape[0]

    def row_copy(idx_ref, buf, r, k):
        return pltpu.make_async_copy(ys_hbm.at[pl.ds(idx_ref[0, TOP_K * r + k], 1)],
                                     ybuf.at[buf, k, pl.ds(r, 1)], sem.at[buf])

    def start_tile(idx_ref, buf):
        def body(r, c):
            for k in range(TOP_K):
                row_copy(idx_ref, buf, r, k).start()
            return c
        lax.fori_loop(0, te, body, 0, unroll=ROW_UNROLL)

    cur = i % 2

    @pl.when(i == 0)
    def _():
        start_tile(slot_ref, 0)

    @pl.when(i + 1 < n)
    def _():
        start_tile(nslot_ref, 1 - cur)

    def wait_body(r, c):
        for k in range(TOP_K):
            row_copy(slot_ref, cur, r, k).wait()
        return c

    lax.fori_loop(0, te, wait_body, 0, unroll=ROW_UNROLL)
    w1 = rt_ref[:, TOP_K:TOP_K + 1]
    w2 = rt_ref[:, TOP_K + 1:TOP_K + 2]
    x2 = x_ref[...] + g2_ref[...] * (w1 * ybuf[cur, 0] + w2 * ybuf[cur, 1])
    x2_ref[...] = x2
    h = _rms(x2) * g_ref[...]
    if modulate:
        h = h * (1.0 + sc_ref[...]) + sh_ref[...]
    h_ref[...] = h.astype(h_ref.dtype)


def moe_combine(x1, ys, slot, route, g, mods_prev, mods_next, seg_ids, te, modulate, h_dtype):
    nt, d = x1.shape
    n = nt // te
    slot3 = slot.reshape(n, 1, TOP_K * te)
    gs = pltpu.PrefetchScalarGridSpec(
        num_scalar_prefetch=1, grid=(n,),
        in_specs=[pl.BlockSpec((None, 1, TOP_K * te), lambda i, s: (i, 0, 0), memory_space=pltpu.SMEM),
                  pl.BlockSpec((None, 1, TOP_K * te), lambda i, s: (jnp.minimum(i + 1, n - 1), 0, 0),
                               memory_space=pltpu.SMEM),
                  pl.BlockSpec((te, d), lambda i, s: (i, 0)),
                  pl.BlockSpec((te, LANES), lambda i, s: (i, 0)),
                  pl.BlockSpec((None, 1, d), lambda i, s: (s[i], 0, 5)),
                  pl.BlockSpec((1, d), lambda i, s: (0, 0)),
                  pl.BlockSpec((None, 1, d), lambda i, s: (s[i], 0, 1)),
                  pl.BlockSpec((None, 1, d), lambda i, s: (s[i], 0, 0)),
                  pl.BlockSpec(memory_space=pl.ANY)],
        out_specs=[pl.BlockSpec((te, d), lambda i, s: (i, 0)),
                   pl.BlockSpec((te, d), lambda i, s: (i, 0))],
        scratch_shapes=[pltpu.VMEM((2, TOP_K, te, d), F32), pltpu.SemaphoreType.DMA((2,))])
    return pl.pallas_call(
        functools.partial(_moe_combine_kernel, modulate=modulate), grid_spec=gs,
        out_shape=[jax.ShapeDtypeStruct((nt, d), F32), jax.ShapeDtypeStruct((nt, d), h_dtype)],
        compiler_params=_cparams(("arbitrary",)),
        name="moe_combine",
    )(seg_ids, slot3, slot3, x1, route, mods_prev, g.reshape(1, d), mods_next, mods_next, ys)


def kernel(x_prompt, x_sample, c, cache_na_k, cache_na_v, cache_gqa_k, cache_gqa_v, c_ctx, norm1_g, norm2_g, final_norm_g, ada_w, ada_b, w_in, w_out, na_rpb, q_norm_g, k_norm_g, hy_conv_w, hy_conv_b, hy_w1, hy_b1, hy_w2, hy_b2, hy_w3, hy_b3, hy_freq, hy_w_out, hy_bias, router_group_w, router_group_b, router_expert_w, router_expert_b, moe_w_gate, moe_w_up, moe_w_down):
    bc, lc, d = x_prompt.shape
    bl, ll, _ = x_sample.shape
    depth = w_in.shape[0]
    past = cache_na_k.shape[2]
    dh = HEAD_DIM
    na_w, gq_w = d // 4, d // 2
    hy_w = d - na_w - gq_w
    na_heads = na_w // dh
    nc = bc * lc
    nt = nc + bl * ll
    tm = 512
    moe_tm = 256
    assert nc % tm == 0 and ll % tm == 0 and past == lc

    te = 256

    def seg_of_tiles(t):
        return jnp.asarray(np.concatenate(
            [np.zeros(nc // t), 1 + np.repeat(np.arange(bl), ll // t)]).astype(np.int32))

    seg_e = seg_of_tiles(te)
    nseg = 1 + bl
    cond = jnp.zeros((8, d), F32).at[0].set(c_ctx).at[1:1 + bl].set(c)
    mods_all = modulation_all(cond, ada_w, ada_b)

    ropes = rope_tables(ll)
    tabs_c = dft_tables(lc)
    tabs_l = dft_tables(ll)
    ck_na = cache_na_k.reshape(bl, depth, past, na_w)
    cv_na = cache_na_v.reshape(bl, depth, past, na_w)
    ck_gq = cache_gqa_k.reshape(bl, depth, past, GQA_KV_HEADS * dh)
    cv_gq = cache_gqa_v.reshape(bl, depth, past, GQA_KV_HEADS * dh)

    x = jnp.concatenate([x_prompt.reshape(nc, d), x_sample.reshape(bl * ll, d)], axis=0)
    mods = mods_all[0, :nseg].reshape(nseg, 1, 6 * d)
    h = norm_mod(x, norm1_g[0], mods, 1, 0, seg_e, te, BF16)
    st_na_k, st_na_v, st_gq_k, st_gq_v = [], [], [], []
    hy_col0 = 3 * na_w + gq_w + 2 * GQA_KV_HEADS * dh
    moe_x = jnp.zeros((nt * TOP_K + N_EXPERTS * moe_tm, d), F32)
    for l in range(depth):
        z = in_proj(h, w_in, l)
        zc = z[:nc]
        st_na_k.append(zc[:, na_w:2 * na_w].reshape(bc, lc, na_heads, dh))
        st_na_v.append(zc[:, 2 * na_w:3 * na_w].reshape(bc, lc, na_heads, dh))
        st_gq_v.append(zc[:, 3 * na_w + gq_w + GQA_KV_HEADS * dh:hy_col0].reshape(bc, lc, GQA_KV_HEADS, dh))

        ctx_a, ctx_b, bk = ctx_attn(z, q_norm_g[l], k_norm_g[l], bc, lc, na_w, gq_w)
        st_gq_k.append(bk.reshape(bc, lc, GQA_KV_HEADS, dh))

        bias = na_bias(na_rpb[l])
        lat_a = lat_na(z, ck_na, cv_na, l, bias, nc, bl, ll, na_w)
        q, k, v = lat_prep(z, ropes, q_norm_g[l], k_norm_g[l], nc, bl, ll, gq_w)
        lat_b = lat_gqa(q, k, v, ck_gq, cv_gq, l)

        mix_c = []
        for (row0, nb, L, tabs) in ((0, bc, lc, tabs_c), (nc, bl, ll, tabs_l)):
            ctab, s1tab, s2tab = tabs
            x0, zz, zb = hy_pre(z, hy_conv_w[l], hy_conv_b[l], row0, nb, L, hy_col0, hy_w)
            filt = hy_filter(L, hy_w1[l], hy_b1[l], hy_w2[l], hy_b2[l], hy_w3[l], hy_b3[l],
                             hy_freq[l], hy_w_out[l])
            kp, kq = dft_fwd(ctab, s1tab, filt)
            xr, xs = dft_fwd(ctab, s1tab, zb)
            yr, yi = spec_mul(xr, xs, kp, kq)
            mix_c.append(dft_inv(ctab, s2tab, yr, yi, x0, zz, hy_bias[l]))

        x1, h2, route = out_proj_router((ctx_a, ctx_b, mix_c[0]), (lat_a, lat_b, mix_c[1]),
                                        w_out[l].astype(BF16), x, norm2_g[l], mods, seg_e, te,
                                        router_group_w[l], router_group_b[l],
                                        router_expert_w[l], router_expert_b[l])
        meta, slot = moe_plan(route, moe_tm)
        moe_x = moe_dispatch(h2, slot, moe_x, te)
        ys = moe_experts(moe_x, meta, moe_w_gate, moe_w_up, moe_w_down, l, moe_tm)
        last = l == depth - 1
        mods_next = mods if last else mods_all[l + 1, :nseg].reshape(nseg, 1, 6 * d)
        g_next = final_norm_g if last else norm1_g[l + 1]
        x, h = moe_combine(x1, ys, slot, route, g_next, mods, mods_next, seg_e, te, not last,
                           F32 if last else BF16)
        mods = mods_next

    y_prompt = h[:nc].reshape(bc, lc, d)
    y_sample = h[nc:].reshape(bl, ll, d)
    return (y_prompt, y_sample, jnp.stack(st_na_k, axis=1), jnp.stack(st_na_v, axis=1),
            jnp.stack(st_gq_k, axis=1), jnp.stack(st_gq_v, axis=1))
```

```python
import functools
import math

import numpy as np
import jax
import jax.numpy as jnp
from jax import lax
from jax.experimental import pallas as pl
from jax.experimental.pallas import tpu as pltpu

F32 = jnp.float32
BF16 = jnp.bfloat16
HIGHEST = lax.Precision.HIGHEST

GRID_W = 64
HEAD_DIM = 128
NA_WIN_R = 8
NA_WIN_C = 16
GQA_KV_HEADS = 2
ROPE_THETA = 10000.0
HY_EMB = 33
HY_HID = 64
HY_FAST_DECAY = 0.3
HY_SLOW_DECAY = 1.5
HY_TARGET = 0.01
N_GROUPS = 4
EXP_PER_GROUP = 4
N_EXPERTS = N_GROUPS * EXP_PER_GROUP
TOP_K = 2
EPS = 1e-6
NEG_INF = -1e30

LANES = 128
VMEM_LIMIT = 52 * 1024 * 1024


def _cparams(sem, vmem=VMEM_LIMIT):
    return pltpu.CompilerParams(dimension_semantics=sem, vmem_limit_bytes=vmem)


def _dot(a, b):
    return jnp.dot(a, b, preferred_element_type=F32)


def _dot_t(a, b):
    return lax.dot_general(a, b, (((1,), (1,)), ((), ())), preferred_element_type=F32)


def _dot_hi(a, b):
    return jnp.dot(a, b, preferred_element_type=F32, precision=HIGHEST)


def _mod_kernel(c_ref, w_ref, b_ref, o_ref):
    c = c_ref[...]
    s = c * (1.0 / (1.0 + jnp.exp(-c)))
    o_ref[...] = _dot_hi(s, w_ref[...]) + b_ref[...]


def modulation_all(cond, ada_w, ada_b, tn=1536):
    depth, d, n = ada_w.shape
    r = cond.shape[0]
    return pl.pallas_call(
        _mod_kernel,
        grid=(depth, n // tn),
        in_specs=[pl.BlockSpec((r, d), lambda l, j: (0, 0)),
                  pl.BlockSpec((None, d, tn), lambda l, j: (l, 0, j)),
                  pl.BlockSpec((None, 1, tn), lambda l, j: (l, 0, j))],
        out_specs=pl.BlockSpec((None, r, tn), lambda l, j: (l, 0, j)),
        out_shape=jax.ShapeDtypeStruct((depth, r, n), F32),
        compiler_params=_cparams(("arbitrary", "arbitrary")),
        name="modulation",
    )(cond, ada_w, ada_b.reshape(depth, 1, n))


def _rms(x):
    return x * lax.rsqrt(jnp.mean(x * x, axis=-1, keepdims=True) + EPS)


def _norm_mod_kernel(s_ref, x_ref, g_ref, sc_ref, sh_ref, o_ref):
    h = _rms(x_ref[...]) * g_ref[...]
    o_ref[...] = (h * (1.0 + sc_ref[...]) + sh_ref[...]).astype(o_ref.dtype)


def norm_mod(x, g, mods, sc_idx, sh_idx, seg_ids, tm, out_dtype):
    nt, d = x.shape
    gs = pltpu.PrefetchScalarGridSpec(
        num_scalar_prefetch=1, grid=(nt // tm,),
        in_specs=[pl.BlockSpec((tm, d), lambda i, s: (i, 0)),
                  pl.BlockSpec((1, d), lambda i, s: (0, 0)),
                  pl.BlockSpec((None, 1, d), lambda i, s: (s[i], 0, sc_idx)),
                  pl.BlockSpec((None, 1, d), lambda i, s: (s[i], 0, sh_idx))],
        out_specs=pl.BlockSpec((tm, d), lambda i, s: (i, 0)))
    return pl.pallas_call(
        _norm_mod_kernel, grid_spec=gs,
        out_shape=jax.ShapeDtypeStruct((nt, d), out_dtype),
        compiler_params=_cparams(("arbitrary",)),
        name="norm_mod",
    )(seg_ids, x, g.reshape(1, d), mods, mods)


def _mm_kernel(a_ref, w_ref, o_ref, wbf_ref):
    @pl.when(pl.program_id(1) == 0)
    def _():
        wbf_ref[...] = w_ref[...].astype(BF16)

    o_ref[...] = _dot(a_ref[...], wbf_ref[...])


def in_proj(h, w_in, layer, tm=1024, tn=768):
    nt, d = h.shape
    tm = math.gcd(tm, nt)
    n = w_in.shape[2]
    return pl.pallas_call(
        _mm_kernel,
        grid=(n // tn, nt // tm),
        in_specs=[pl.BlockSpec((tm, d), lambda j, i: (i, 0)),
                  pl.BlockSpec((None, d, tn), lambda j, i: (layer, 0, j))],
        out_specs=pl.BlockSpec((tm, tn), lambda j, i: (i, j)),
        out_shape=jax.ShapeDtypeStruct((nt, n), F32),
        scratch_shapes=[pltpu.VMEM((d, tn), BF16)],
        compiler_params=_cparams(("arbitrary", "arbitrary")),
        name="in_proj",
    )(h, w_in)


def _softmax_pv(s, v):
    m = jnp.max(s, axis=-1, keepdims=True)
    p = jnp.exp(s - m)
    l = jnp.sum(p, axis=-1, keepdims=True)
    return _dot(p.astype(BF16), v) / l


def _ctx_attn_kernel(za_ref, zb_ref, qg_ref, kg_ref, a_ref, b_ref, bk_ref, *, na_heads, gq_heads):
    dh = HEAD_DIM
    scale = dh ** -0.5
    L = za_ref.shape[0]
    na_w = na_heads * dh
    for h in range(na_heads):
        q = za_ref[:, h * dh:(h + 1) * dh].astype(BF16)
        k = za_ref[:, na_w + h * dh:na_w + (h + 1) * dh].astype(BF16)
        v = za_ref[:, 2 * na_w + h * dh:2 * na_w + (h + 1) * dh].astype(BF16)
        s = _dot_t(q, k) * scale
        a_ref[:, h * dh:(h + 1) * dh] = _softmax_pv(s, v).astype(a_ref.dtype)
    gq_w = gq_heads * dh
    group = gq_heads // GQA_KV_HEADS
    for kv in range(GQA_KV_HEADS):
        kn = _rms(zb_ref[:, gq_w + kv * dh:gq_w + (kv + 1) * dh]) * kg_ref[...]
        bk_ref[:, kv * dh:(kv + 1) * dh] = kn
        knb = kn.astype(BF16)
        v = zb_ref[:, gq_w + (GQA_KV_HEADS + kv) * dh:gq_w + (GQA_KV_HEADS + kv + 1) * dh].astype(BF16)
        for g in range(group):
            h = kv * group + g
            qn = (_rms(zb_ref[:, h * dh:(h + 1) * dh]) * qg_ref[...]).astype(BF16)
            s = _dot_t(qn, knb) * scale
            b_ref[:, h * dh:(h + 1) * dh] = _softmax_pv(s, v).astype(b_ref.dtype)


def ctx_attn(z, q_g, k_g, bc, lc, na_w, gq_w):
    dh = HEAD_DIM
    wa = 3 * na_w
    wb = gq_w + 2 * GQA_KV_HEADS * dh
    assert wa == wb
    kern = functools.partial(_ctx_attn_kernel, na_heads=na_w // dh, gq_heads=gq_w // dh)
    return pl.pallas_call(
        kern, grid=(bc,),
        in_specs=[pl.BlockSpec((lc, wa), lambda b: (b, 0)),
                  pl.BlockSpec((lc, wb), lambda b: (b, 1)),
                  pl.BlockSpec((1, dh), lambda b: (0, 0)),
                  pl.BlockSpec((1, dh), lambda b: (0, 0))],
        out_specs=[pl.BlockSpec((lc, na_w), lambda b: (b, 0)),
                   pl.BlockSpec((lc, gq_w), lambda b: (b, 0)),
                   pl.BlockSpec((lc, GQA_KV_HEADS * dh), lambda b: (b, 0))],
        out_shape=[jax.ShapeDtypeStruct((bc * lc, na_w), BF16),
                   jax.ShapeDtypeStruct((bc * lc, gq_w), BF16),
                   jax.ShapeDtypeStruct((bc * lc, GQA_KV_HEADS * dh), F32)],
        compiler_params=_cparams(("arbitrary",)),
        name="ctx_attn",
    )(z, z, q_g.reshape(1, dh), k_g.reshape(1, dh))


def _rope_kernel(cos_ref, sa_ref, sb_ref):
    n, dh = cos_ref.shape
    quarter = dh // 2
    t = lax.broadcasted_iota(jnp.int32, (n, dh), 0)
    lane = lax.broadcasted_iota(jnp.int32, (n, dh), 1)
    f = (lane & (quarter // 2 - 1)).astype(F32)
    inv = jnp.exp(-(2.0 * f / quarter) * math.log(ROPE_THETA))
    pos = jnp.where(lane < quarter, t >> int(math.log2(GRID_W)), t & (GRID_W - 1)).astype(F32)
    ang = pos * inv
    sin = jnp.sin(ang)
    first = (lane & (quarter - 1)) < (quarter // 2)
    cos_ref[...] = jnp.cos(ang)
    sa_ref[...] = jnp.where(first, -sin, 0.0)
    sb_ref[...] = jnp.where(first, 0.0, sin)


def rope_tables(n):
    shp = jax.ShapeDtypeStruct((n, HEAD_DIM), F32)
    return pl.pallas_call(_rope_kernel, out_shape=[shp, shp, shp], name="rope_tables")()


def _rope(x, cos, sa, sb):
    q4 = HEAD_DIM // 4
    return x * cos + pltpu.roll(x, HEAD_DIM - q4, 1) * sa + pltpu.roll(x, q4, 1) * sb


def _lat_prep_kernel(z_ref, cos_ref, sa_ref, sb_ref, qg_ref, kg_ref, q_ref, k_ref, v_ref, *, gq_heads):
    dh = HEAD_DIM
    scale = dh ** -0.5
    cos, sa, sb = cos_ref[...], sa_ref[...], sb_ref[...]
    for h in range(gq_heads):
        qn = _rms(z_ref[:, h * dh:(h + 1) * dh]) * qg_ref[...]
        q_ref[h] = (_rope(qn, cos, sa, sb) * scale).astype(BF16)
    gq_w = gq_heads * dh
    for kv in range(GQA_KV_HEADS):
        kn = _rms(z_ref[:, gq_w + kv * dh:gq_w + (kv + 1) * dh]) * kg_ref[...]
        k_ref[kv] = _rope(kn, cos, sa, sb).astype(BF16)
        v_ref[kv] = z_ref[:, gq_w + (GQA_KV_HEADS + kv) * dh:gq_w + (GQA_KV_HEADS + kv + 1) * dh].astype(BF16)


def lat_prep(z, ropes, q_g, k_g, nc, bl, ll, gq_w, tl=512):
    dh = HEAD_DIM
    gq_heads = gq_w // dh
    wb = gq_w + 2 * GQA_KV_HEADS * dh
    nb = ll // tl
    off = nc // tl
    cos, sa, sb = ropes
    tab = pl.BlockSpec((tl, dh), lambda b, i: (i, 0))
    one = pl.BlockSpec((1, dh), lambda b, i: (0, 0))
    return pl.pallas_call(
        functools.partial(_lat_prep_kernel, gq_heads=gq_heads),
        grid=(bl, nb),
        in_specs=[pl.BlockSpec((tl, wb), lambda b, i: (off + b * nb + i, 1)), tab, tab, tab, one, one],
        out_specs=[pl.BlockSpec((None, gq_heads, tl, dh), lambda b, i: (b, 0, i, 0)),
                   pl.BlockSpec((None, GQA_KV_HEADS, tl, dh), lambda b, i: (b, 0, i, 0)),
                   pl.BlockSpec((None, GQA_KV_HEADS, tl, dh), lambda b, i: (b, 0, i, 0))],
        out_shape=[jax.ShapeDtypeStruct((bl, gq_heads, ll, dh), BF16),
                   jax.ShapeDtypeStruct((bl, GQA_KV_HEADS, ll, dh), BF16),
                   jax.ShapeDtypeStruct((bl, GQA_KV_HEADS, ll, dh), BF16)],
        compiler_params=_cparams(("arbitrary", "arbitrary")),
        name="lat_prep",
    )(z, cos, sa, sb, q_g.reshape(1, dh), k_g.reshape(1, dh))


def _lat_gqa_kernel(q_ref, k_ref, v_ref, ck_ref, cv_ref, o_ref, *, tk):
    group, tq, dh = q_ref.shape
    rows = group * tq
    q = q_ref[...].reshape(rows, dh)
    ll = k_ref.shape[0]

    def update(carry, kc, vc):
        m, l, acc = carry
        s = _dot_t(q, kc)
        m_new = jnp.maximum(m, jnp.max(s, axis=-1, keepdims=True))
        alpha = jnp.exp(m - m_new)
        p = jnp.exp(s - m_new)
        l = alpha * l + jnp.sum(p, axis=-1, keepdims=True)
        acc = alpha * acc + _dot(p.astype(BF16), vc)
        return m_new, l, acc

    def body(c, carry):
        st = pl.multiple_of(c * tk, tk)
        return update(carry, k_ref[pl.ds(st, tk), :], v_ref[pl.ds(st, tk), :])

    init = (jnp.full((rows, 1), -jnp.inf, F32), jnp.zeros((rows, 1), F32), jnp.zeros((rows, dh), F32))
    carry = lax.fori_loop(0, ll // tk, body, init)
    m, l, acc = update(carry, ck_ref[...].astype(BF16), cv_ref[...].astype(BF16))
    out = acc / l
    for g in range(group):
        o_ref[:, g * dh:(g + 1) * dh] = out[g * tq:(g + 1) * tq].astype(o_ref.dtype)


def lat_gqa(q, k, v, cache_k, cache_v, layer, tq=256, tk=4096):
    bl, gq_heads, ll, dh = q.shape
    group = gq_heads // GQA_KV_HEADS
    past = cache_k.shape[2]
    nq = ll // tq
    tk = min(tk, ll)
    assert ll % tk == 0
    ck = pl.BlockSpec((None, None, past, dh), lambda b, h, i: (b, layer, 0, h))
    return pl.pallas_call(
        functools.partial(_lat_gqa_kernel, tk=tk),
        grid=(bl, GQA_KV_HEADS, nq),
        in_specs=[pl.BlockSpec((None, group, tq, dh), lambda b, h, i: (b, h, i, 0)),
                  pl.BlockSpec((None, None, ll, dh), lambda b, h, i: (b, h, 0, 0)),
                  pl.BlockSpec((None, None, ll, dh), lambda b, h, i: (b, h, 0, 0)),
                  ck, ck],
        out_specs=pl.BlockSpec((tq, group * dh), lambda b, h, i: (b * nq + i, h)),
        out_shape=jax.ShapeDtypeStruct((bl * ll, gq_heads * dh), BF16),
        compiler_params=_cparams(("arbitrary", "arbitrary", "arbitrary")),
        name="lat_gqa",
    )(q, k, v, cache_k, cache_v)


def _na_bias_kernel(rpb_ref, o_ref):
    h = pl.program_id(0)
    w = GRID_W
    qc = lax.broadcasted_iota(jnp.int32, (w, w), 0)
    kc = lax.broadcasted_iota(jnp.int32, (w, w), 1)
    ci = jnp.clip(kc - qc + NA_WIN_C - 1, 0, 2 * NA_WIN_C - 2)
    cs = jnp.clip(qc - NA_WIN_C // 2, 0, w - NA_WIN_C)
    col_ok = (kc >= cs) & (kc < cs + NA_WIN_C)
    nr, ncol = 2 * NA_WIN_R - 1, 2 * NA_WIN_C - 1
    tiles = []
    for dr in range(nr):
        t = jnp.zeros((w, w), F32)
        for c in range(ncol):
            t = jnp.where(ci == c, rpb_ref[h, dr * ncol + c], t)
        tiles.append(jnp.where(col_ok, t, NEG_INF))
    for d0 in range(NA_WIN_R):
        for j in range(NA_WIN_R):
            o_ref[d0, :, j * w:(j + 1) * w] = tiles[d0 + j]


def na_bias(rpb_l):
    heads = rpb_l.shape[0]
    flat = rpb_l.reshape(heads, -1)
    return pl.pallas_call(
        _na_bias_kernel, grid=(heads,),
        in_specs=[pl.BlockSpec(memory_space=pltpu.SMEM)],
        out_specs=pl.BlockSpec((None, NA_WIN_R, GRID_W, NA_WIN_R * GRID_W), lambda h: (h, 0, 0, 0)),
        out_shape=jax.ShapeDtypeStruct((heads, NA_WIN_R, GRID_W, NA_WIN_R * GRID_W), F32),
        compiler_params=_cparams(("arbitrary",)),
        name="na_bias",
    )(flat)


def _lat_na_kernel(q_ref, k_ref, v_ref, ck_ref, cv_ref, bias_ref, o_ref, kb_ref, vb_ref, *, rb, rows):
    w = GRID_W
    scale = HEAD_DIM ** -0.5
    i = pl.program_id(2)

    @pl.when(i == 0)
    def _():
        kb_ref[...] = k_ref[...].astype(BF16)
        vb_ref[...] = v_ref[...].astype(BF16)

    ck = ck_ref[...].astype(BF16)
    cv = cv_ref[...].astype(BF16)
    qs = (q_ref[...] * scale).astype(BF16)
    s2 = _dot_t(qs, ck)
    s1_rows, v_wins = [], []
    for j in range(rb):
        r = i * rb + j
        rs = jnp.clip(r - NA_WIN_R // 2, 0, rows - NA_WIN_R)
        d0 = rs - r + NA_WIN_R - 1
        st = pl.multiple_of(rs * w, w)
        kw = kb_ref[pl.ds(st, NA_WIN_R * w), :]
        v_wins.append(vb_ref[pl.ds(st, NA_WIN_R * w), :])
        s1_rows.append(_dot_t(qs[j * w:(j + 1) * w], kw) + bias_ref[d0])
    s1 = jnp.concatenate(s1_rows, axis=0)
    m = jnp.maximum(jnp.max(s1, axis=-1, keepdims=True), jnp.max(s2, axis=-1, keepdims=True))
    p1 = jnp.exp(s1 - m)
    p2 = jnp.exp(s2 - m)
    l = jnp.sum(p1, axis=-1, keepdims=True) + jnp.sum(p2, axis=-1, keepdims=True)
    p1 = p1.astype(BF16)
    o2 = _dot(p2.astype(BF16), cv)
    o1 = jnp.concatenate([_dot(p1[j * w:(j + 1) * w], v_wins[j]) for j in range(rb)], axis=0)
    o_ref[...] = ((o1 + o2) / l).astype(o_ref.dtype)


def lat_na(z, cache_k, cache_v, layer, bias, nc, bl, ll, na_w, rb=32):
    dh = HEAD_DIM
    heads = na_w // dh
    rows = ll // GRID_W
    rb = min(rb, rows)
    assert rows >= NA_WIN_R and nc % ll == 0 and rows % rb == 0
    past = cache_k.shape[2]
    nb = rows // rb
    tq = rb * GRID_W
    off = nc // tq
    lat0 = nc // ll
    ck = pl.BlockSpec((None, None, past, dh), lambda b, h, i: (b, layer, 0, h))
    return pl.pallas_call(
        functools.partial(_lat_na_kernel, rb=rb, rows=rows),
        grid=(bl, heads, nb),
        in_specs=[pl.BlockSpec((tq, dh), lambda b, h, i: (off + b * nb + i, h)),
                  pl.BlockSpec((ll, dh), lambda b, h, i: (lat0 + b, heads + h)),
                  pl.BlockSpec((ll, dh), lambda b, h, i: (lat0 + b, 2 * heads + h)),
                  ck, ck,
                  pl.BlockSpec((None, NA_WIN_R, GRID_W, NA_WIN_R * GRID_W), lambda b, h, i: (h, 0, 0, 0))],
        out_specs=pl.BlockSpec((tq, dh), lambda b, h, i: (b * nb + i, h)),
        out_shape=jax.ShapeDtypeStruct((bl * ll, na_w), BF16),
        scratch_shapes=[pltpu.VMEM((ll, dh), BF16), pltpu.VMEM((ll, dh), BF16)],
        compiler_params=_cparams(("arbitrary", "arbitrary", "arbitrary")),
        name="lat_na",
    )(z, z, z, cache_k, cache_v, bias)


def _hy_pre_kernel(u0_ref, u1_ref, u2_ref, w0_ref, w1_ref, w2_ref, b0_ref, b1_ref, b2_ref,
                   x0_ref, zz_ref, zb_ref):
    L = u0_ref.shape[0]
    t = lax.broadcasted_iota(jnp.int32, u0_ref.shape, 0)

    def conv(u_ref, w_ref, b_ref):
        u = u_ref[...]
        prev = jnp.where(t == 0, 0.0, pltpu.roll(u, 1, 0))
        nxt = jnp.where(t == L - 1, 0.0, pltpu.roll(u, L - 1, 0))
        return prev * w_ref[0:1, :] + u * w_ref[1:2, :] + nxt * w_ref[2:3, :] + b_ref[...]

    x0_ref[...] = conv(u0_ref, w0_ref, b0_ref)
    zz = conv(u2_ref, w2_ref, b2_ref) * conv(u1_ref, w1_ref, b1_ref)
    zz_ref[...] = zz
    zb_ref[...] = zz.astype(BF16)


def hy_pre(z, conv_w, conv_b, row0, nb, L, col0, hy_w):
    cb = LANES
    nj = hy_w // cb
    c0 = col0 // cb
    r0 = row0 // L
    assert row0 % L == 0
    u = lambda part: pl.BlockSpec((L, cb), lambda b, j: (r0 + b, c0 + part * nj + j))
    wspec = lambda part: pl.BlockSpec((3, cb), lambda b, j: (0, part * nj + j))
    bspec = lambda part: pl.BlockSpec((1, cb), lambda b, j: (0, part * nj + j))
    o = pl.BlockSpec((None, L, cb), lambda b, j: (b, 0, j))
    return pl.pallas_call(
        _hy_pre_kernel, grid=(nb, nj),
        in_specs=[u(0), u(1), u(2), wspec(0), wspec(1), wspec(2), bspec(0), bspec(1), bspec(2)],
        out_specs=[o, o, o],
        out_shape=[jax.ShapeDtypeStruct((nb, L, hy_w), F32), jax.ShapeDtypeStruct((nb, L, hy_w), F32),
                   jax.ShapeDtypeStruct((nb, L, hy_w), BF16)],
        compiler_params=_cparams(("arbitrary", "arbitrary")),
        name="hy_pre",
    )(z, z, z, conv_w, conv_w, conv_w, conv_b.reshape(1, -1), conv_b.reshape(1, -1), conv_b.reshape(1, -1))


def _hy_filter_kernel(w1_ref, b1_ref, w2_ref, b2_ref, w3_ref, b3_ref, fq_ref, wo_ref, o_ref, *, hy_w):
    L = o_ref.shape[1]
    bands = (HY_EMB - 1) // 2
    ti = lax.broadcasted_iota(jnp.int32, (L, LANES), 0).astype(F32)
    lane = lax.broadcasted_iota(jnp.int32, (L, LANES), 1)
    t01 = ti / (L - 1)
    w = (2.0 * math.pi / L) * ti
    band = ((lane - 1) % bands).astype(F32)
    fr = 1e-4 + band * ((bands - 1 - 1e-4) / (bands - 1))
    ang = fr * w
    feat = jnp.where(lane == 0, t01,
                     jnp.where(lane <= bands, jnp.cos(ang),
                               jnp.where(lane <= 2 * bands, -jnp.sin(ang), 0.0)))
    fq = fq_ref[...]
    hdn = jnp.sin(fq * (_dot_hi(feat, w1_ref[...]) + b1_ref[...]))
    hdn = jnp.sin(fq * (_dot_hi(hdn, w2_ref[...]) + b2_ref[...]))
    hdn = jnp.sin(fq * (_dot_hi(hdn, w3_ref[...]) + b3_ref[...]))
    filt = _dot_hi(hdn, wo_ref[...])
    max_decay = math.log(HY_TARGET) / HY_FAST_DECAY
    min_decay = math.log(HY_TARGET) / HY_SLOW_DECAY
    ch = lax.broadcasted_iota(jnp.int32, (L, hy_w), 1).astype(F32)
    deltas = jnp.abs(min_decay + ch * ((max_decay - min_decay) / (hy_w - 1)))
    tt = lax.broadcasted_iota(jnp.int32, (L, hy_w), 0)
    decay = jnp.exp(-(tt.astype(F32) / (L - 1)) * deltas)
    o_ref[0] = (filt[:, :hy_w] * decay).astype(o_ref.dtype)
    o_ref[1] = jnp.where(tt == 0, 0.0, filt[:, hy_w:] * decay).astype(o_ref.dtype)


def _pad2(a, r, c):
    return jnp.pad(a, ((0, r - a.shape[0]), (0, c - a.shape[1])))


def hy_filter(L, w1, b1, w2, b2, w3, b3, freq, w_out):
    hy_w = w_out.shape[1] // 2
    p = LANES
    args = (_pad2(w1, p, p), _pad2(b1[None], 1, p), _pad2(w2, p, p), _pad2(b2[None], 1, p),
            _pad2(w3, p, p), _pad2(b3[None], 1, p), _pad2(freq[None], 1, p), _pad2(w_out, p, 2 * hy_w))
    return pl.pallas_call(
        functools.partial(_hy_filter_kernel, hy_w=hy_w),
        out_shape=jax.ShapeDtypeStruct((2, L, hy_w), BF16),
        compiler_params=_cparams(None),
        name="hy_filter",
    )(*args)


def _dft_tables_kernel(c_ref, s1_ref, s2_ref, *, L):
    tk = c_ref.shape[0]
    n2 = 2 * L
    theta = 2.0 * math.pi / n2
    k = pl.program_id(0) * tk + lax.broadcasted_iota(jnp.int32, (tk, LANES), 0)
    lane = lax.broadcasted_iota(jnp.int32, (tk, LANES), 1)
    a = ((k * lane) & (n2 - 1)).astype(F32) * theta
    b = ((k * LANES * lane) & (n2 - 1)).astype(F32) * theta
    ca, sa, cb, sb = jnp.cos(a), jnp.sin(a), jnp.cos(b), jnp.sin(b)
    sign_k = (1 - 2 * (k & 1)).astype(F32)
    for j in range(L // LANES):
        cbj = cb[:, j:j + 1]
        sbj = sb[:, j:j + 1]
        c = ca * cbj - sa * sbj
        s = sa * cbj + ca * sbj
        t = lane + j * LANES
        sign_t = (1 - 2 * (t & 1)).astype(F32)
        c_ref[:, j * LANES:(j + 1) * LANES] = c.astype(BF16)
        s1_ref[:, j * LANES:(j + 1) * LANES] = jnp.where(k == 0, sign_t, s).astype(BF16)
        s2_ref[:, j * LANES:(j + 1) * LANES] = jnp.where(t == 0, sign_k, s).astype(BF16)


def dft_tables(L, tk=256):
    tk = min(tk, L)
    shp = jax.ShapeDtypeStruct((L, L), BF16)
    spec = pl.BlockSpec((tk, L), lambda i: (i, 0))
    return pl.pallas_call(
        functools.partial(_dft_tables_kernel, L=L), grid=(L // tk,),
        out_specs=[spec, spec, spec], out_shape=[shp, shp, shp],
        compiler_params=_cparams(("arbitrary",)),
        name="dft_tables",
    )()


def _dft_fwd_kernel(c_ref, s_ref, z_ref, re_ref, im_ref):
    z = z_ref[...]
    re_ref[...] = _dot(c_ref[...], z)
    im_ref[...] = _dot(s_ref[...], z)


def dft_fwd(ctab, stab, zb, tm=512):
    nb, L, cw = zb.shape
    tm = min(tm, L)
    tab = pl.BlockSpec((tm, L), lambda i, b: (i, 0))
    o = pl.BlockSpec((None, tm, cw), lambda i, b: (b, i, 0))
    shp = jax.ShapeDtypeStruct((nb, L, cw), F32)
    return pl.pallas_call(
        _dft_fwd_kernel, grid=(L // tm, nb),
        in_specs=[tab, tab, pl.BlockSpec((None, L, cw), lambda i, b: (b, 0, 0))],
        out_specs=[o, o], out_shape=[shp, shp],
        compiler_params=_cparams(("arbitrary", "arbitrary")),
        name="dft_fwd",
    )(ctab, stab, zb)


def _spec_mul_kernel(xr_ref, xs_ref, kp_ref, kq_ref, yr_ref, yi_ref, *, n2):
    tl = xr_ref.shape[0]
    k = pl.program_id(1) * tl + lax.broadcasted_iota(jnp.int32, xr_ref.shape, 0)
    xr, xs = xr_ref[...], xs_ref[...]
    kre = kp_ref[0] + kp_ref[1]
    kim = kq_ref[1] - kq_ref[0]
    knyq = kq_ref[0] + kq_ref[1]
    first = k == 0
    yre = xr * kre + jnp.where(first, 0.0, xs * kim)
    yim = xr * kim - xs * kre
    yr_ref[...] = (jnp.where(first, 1.0 / n2, 2.0 / n2) * yre).astype(BF16)
    yi_ref[...] = jnp.where(first, xs * knyq * (1.0 / n2), (-2.0 / n2) * yim).astype(BF16)


def spec_mul(xr, xs, kp, kq, tl=512):
    nb, L, cw = xr.shape
    tl = min(tl, L)
    d = pl.BlockSpec((None, tl, cw), lambda b, i: (b, i, 0))
    f = pl.BlockSpec((2, tl, cw), lambda b, i: (0, i, 0))
    shp = jax.ShapeDtypeStruct((nb, L, cw), BF16)
    return pl.pallas_call(
        functools.partial(_spec_mul_kernel, n2=2 * L), grid=(nb, L // tl),
        in_specs=[d, d, f, f], out_specs=[d, d], out_shape=[shp, shp],
        compiler_params=_cparams(("arbitrary", "arbitrary")),
        name="spec_mul",
    )(xr, xs, kp, kq)


def _dft_inv_kernel(c_ref, s_ref, yr_ref, yi_ref, x0_ref, zz_ref, bd_ref, o_ref):
    y = _dot(c_ref[...], yr_ref[...]) + _dot(s_ref[...], yi_ref[...])
    o_ref[...] = (x0_ref[...] * (y + zz_ref[...] * bd_ref[...])).astype(o_ref.dtype)


def dft_inv(ctab, s2tab, yr, yi, x0, zz, bias_d, tm=512):
    nb, L, cw = yr.shape
    tm = min(tm, L)
    nti = L // tm
    tab = pl.BlockSpec((tm, L), lambda i, b: (i, 0))
    full = pl.BlockSpec((None, L, cw), lambda i, b: (b, 0, 0))
    tile = pl.BlockSpec((None, tm, cw), lambda i, b: (b, i, 0))
    return pl.pallas_call(
        _dft_inv_kernel, grid=(nti, nb),
        in_specs=[tab, tab, full, full, tile, tile, pl.BlockSpec((1, cw), lambda i, b: (0, 0))],
        out_specs=pl.BlockSpec((tm, cw), lambda i, b: (b * nti + i, 0)),
        out_shape=jax.ShapeDtypeStruct((nb * L, cw), BF16),
        compiler_params=_cparams(("arbitrary", "arbitrary")),
        name="dft_inv",
    )(ctab, s2tab, yr, yi, x0, zz, bias_d.reshape(1, cw))


def _split_bf16(a):
    hi = a.astype(BF16)
    return hi, (a - hi.astype(F32)).astype(BF16)


def _dot_3pass(a, b):
    a_hi, a_lo = _split_bf16(a)
    b_hi, b_lo = _split_bf16(b)
    return _dot(a_hi, b_hi) + (_dot(a_lo, b_hi) + _dot(a_hi, b_lo))


def _route(logits):
    li = lax.broadcasted_iota(jnp.int32, logits.shape, 1)
    big = jnp.int32(LANES)
    ninf = -jnp.inf

    def rmax(x):
        return jnp.max(x, axis=-1, keepdims=True)

    def first_at(x, m):
        return jnp.min(jnp.where(x == m, li, big), axis=-1, keepdims=True)

    gmask = li < N_GROUPS
    gl = jnp.where(gmask, logits, ninf)
    gmax = rmax(gl)
    g_sel = first_at(gl, gmax)
    g_prob = 1.0 / jnp.sum(jnp.where(gmask, jnp.exp(gl - gmax), 0.0), axis=-1, keepdims=True)
    e = li - N_GROUPS
    emask = (e >= 0) & (e < N_EXPERTS) & ((e >> 2) == g_sel)
    el = jnp.where(emask, logits, ninf)
    emax = rmax(el)
    esum = jnp.sum(jnp.where(emask, jnp.exp(el - emax), 0.0), axis=-1, keepdims=True)
    l1 = first_at(el, emax)
    el2 = jnp.where(li == l1, ninf, el)
    emax2 = rmax(el2)
    l2 = first_at(el2, emax2)
    p1 = 1.0 / esum
    p2 = jnp.exp(emax2 - emax) / esum
    tot = p1 + p2
    w1 = p1 / tot * g_prob
    w2 = p2 / tot * g_prob
    return jnp.where(li == 0, (l1 - N_GROUPS).astype(F32),
                     jnp.where(li == 1, (l2 - N_GROUPS).astype(F32),
                               jnp.where(li == 2, w1, jnp.where(li == 3, w2, 0.0))))


def _out_router_kernel(s_ref, ca_ref, cb_ref, cc_ref, la_ref, lb_ref, lc_ref, w_ref, x_ref, g1_ref,
                       g_ref, sc_ref, sh_ref, wr_ref, br_ref, x1_ref, h_ref, rt_ref, *, nci):
    i = pl.program_id(0)

    def project(a_ref, b_ref, c_ref):
        wa = a_ref.shape[1]
        wb = b_ref.shape[1]
        acc = _dot(a_ref[...], w_ref[0:wa, :])
        acc += _dot(b_ref[...], w_ref[wa:wa + wb, :])
        acc += _dot(c_ref[...], w_ref[wa + wb:, :])
        x1_ref[...] = x_ref[...] + g1_ref[...] * acc

    @pl.when(i < nci)
    def _():
        project(ca_ref, cb_ref, cc_ref)

    @pl.when(i >= nci)
    def _():
        project(la_ref, lb_ref, lc_ref)

    h = _rms(x1_ref[...]) * g_ref[...]
    h = h * (1.0 + sc_ref[...]) + sh_ref[...]
    h_ref[...] = h
    rt_ref[...] = _route(_dot_3pass(h, wr_ref[...]) + br_ref[...])


def out_proj_router(mix_ctx, mix_lat, w_out_bf, x, g, mods, seg_ids, tm, wg, bg, we, be):
    nt, d = x.shape
    nci = mix_ctx[0].shape[0] // tm
    wr = _pad2(jnp.concatenate([wg, we], axis=1), d, LANES)
    br = _pad2(jnp.concatenate([bg, be])[None], 1, LANES)
    cspec = lambda a: pl.BlockSpec((tm, a.shape[1]), lambda i, s: (jnp.minimum(i, nci - 1), 0))
    lspec = lambda a: pl.BlockSpec((tm, a.shape[1]), lambda i, s: (jnp.maximum(i - nci, 0), 0))
    row = pl.BlockSpec((tm, d), lambda i, s: (i, 0))
    mod = lambda p: pl.BlockSpec((None, 1, d), lambda i, s: (s[i], 0, p))
    gs = pltpu.PrefetchScalarGridSpec(
        num_scalar_prefetch=1, grid=(nt // tm,),
        in_specs=[cspec(mix_ctx[0]), cspec(mix_ctx[1]), cspec(mix_ctx[2]),
                  lspec(mix_lat[0]), lspec(mix_lat[1]), lspec(mix_lat[2]),
                  pl.BlockSpec((d, d), lambda i, s: (0, 0)),
                  row, mod(2),
                  pl.BlockSpec((1, d), lambda i, s: (0, 0)), mod(4), mod(3),
                  pl.BlockSpec((d, LANES), lambda i, s: (0, 0)),
                  pl.BlockSpec((1, LANES), lambda i, s: (0, 0))],
        out_specs=[row, row, pl.BlockSpec((tm, LANES), lambda i, s: (i, 0))])
    return pl.pallas_call(
        functools.partial(_out_router_kernel, nci=nci), grid_spec=gs,
        out_shape=[jax.ShapeDtypeStruct((nt, d), F32), jax.ShapeDtypeStruct((nt, d), F32),
                   jax.ShapeDtypeStruct((nt, LANES), F32)],
        compiler_params=_cparams(("arbitrary",)),
        name="out_proj_router",
    )(seg_ids, *mix_ctx, *mix_lat, w_out_bf, x, mods, g.reshape(1, d), mods, mods, wr, br)


def moe_plan(route, tm):
    nt = route.shape[0]
    npair = nt * TOP_K
    ns = npair + N_EXPERTS * tm
    n_tiles = ns // tm
    e = route[:, :TOP_K].astype(jnp.int32).reshape(npair)
    onehot = (e[:, None] == jnp.arange(N_EXPERTS, dtype=jnp.int32)[None, :]).astype(jnp.int32)
    csum = jnp.cumsum(onehot, axis=0)
    counts = csum[-1]
    rank = jnp.sum((csum - onehot) * onehot, axis=1)
    padded = ((counts + tm - 1) // tm) * tm
    ends = jnp.cumsum(padded)
    starts = ends - padded
    slot = (jnp.sum(onehot * starts[None, :], axis=1) + rank).astype(jnp.int32)
    tile_start = jnp.arange(n_tiles, dtype=jnp.int32) * tm
    tile_e = jnp.sum((tile_start[:, None] >= ends[None, :]).astype(jnp.int32), axis=1)
    tile_valid = (tile_start < ends[-1]).astype(jnp.int32)
    last_e = jnp.max(jnp.where(counts > 0, jnp.arange(N_EXPERTS, dtype=jnp.int32), 0))
    tile_e = jnp.where(tile_valid > 0, tile_e, last_e).astype(jnp.int32)
    prev = jnp.concatenate([jnp.full((1,), -1, jnp.int32), tile_e[:-1]])
    tile_new = (tile_e != prev).astype(jnp.int32)
    meta = jnp.stack([tile_e, tile_valid, tile_new], axis=0)
    return meta, slot


ROW_UNROLL = 8


def _dispatch_kernel(slot_ref, h_ref, xs_in, xs_hbm, sem):
    del xs_in
    te = h_ref.shape[0]

    def row_copy(r, k):
        return pltpu.make_async_copy(h_ref.at[pl.ds(r, 1)], xs_hbm.at[pl.ds(slot_ref[0, TOP_K * r + k], 1)], sem)

    def start(r, c):
        for k in range(TOP_K):
            row_copy(r, k).start()
        return c

    lax.fori_loop(0, te, start, 0, unroll=ROW_UNROLL)

    def wait(r, c):
        for k in range(TOP_K):
            row_copy(r, k).wait()
        return c

    lax.fori_loop(0, te, wait, 0, unroll=ROW_UNROLL)


def moe_dispatch(h, slot, xs_prev, te):
    nt, d = h.shape
    n = nt // te
    ns = xs_prev.shape[0]
    return pl.pallas_call(
        _dispatch_kernel, grid=(n,),
        in_specs=[pl.BlockSpec((None, 1, TOP_K * te), lambda i: (i, 0, 0), memory_space=pltpu.SMEM),
                  pl.BlockSpec((te, d), lambda i: (i, 0)),
                  pl.BlockSpec(memory_space=pl.ANY)],
        out_specs=pl.BlockSpec(memory_space=pl.ANY),
        out_shape=jax.ShapeDtypeStruct((ns, d), F32),
        scratch_shapes=[pltpu.SemaphoreType.DMA],
        input_output_aliases={2: 0},
        compiler_params=_cparams(("arbitrary",)),
        name="moe_dispatch",
    )(slot.reshape(n, 1, TOP_K * te), h, xs_prev)


def _moe_kernel(meta_ref, x_ref, wg_ref, wu_ref, wd_ref, y_ref, wgb, wub, wdb):
    t = pl.program_id(0)

    @pl.when(meta_ref[1, t] > 0)
    def _():
        @pl.when(meta_ref[2, t] > 0)
        def _():
            wgb[...] = wg_ref[...].astype(BF16)
            wub[...] = wu_ref[...].astype(BF16)
            wdb[...] = wd_ref[...].astype(BF16)

        x = x_ref[...].astype(BF16)
        a = _dot(x, wgb[...])
        u = _dot(x, wub[...])
        hid = a * (1.0 / (1.0 + jnp.exp(-a))) * u
        y_ref[...] = _dot(hid.astype(BF16), wdb[...])

    @pl.when(meta_ref[1, t] == 0)
    def _():
        y_ref[...] = jnp.zeros_like(y_ref)


def moe_experts(xs, meta, w_gate, w_up, w_down, layer, tm):
    ns, d = xs.shape
    de = w_gate.shape[3]
    gs = pltpu.PrefetchScalarGridSpec(
        num_scalar_prefetch=1, grid=(ns // tm,),
        in_specs=[pl.BlockSpec((tm, d), lambda t, m: (t, 0)),
                  pl.BlockSpec((None, None, d, de), lambda t, m: (layer, m[0, t], 0, 0)),
                  pl.BlockSpec((None, None, d, de), lambda t, m: (layer, m[0, t], 0, 0)),
                  pl.BlockSpec((None, None, de, d), lambda t, m: (layer, m[0, t], 0, 0))],
        out_specs=pl.BlockSpec((tm, d), lambda t, m: (t, 0)),
        scratch_shapes=[pltpu.VMEM((d, de), BF16), pltpu.VMEM((d, de), BF16), pltpu.VMEM((de, d), BF16)])
    return pl.pallas_call(
        _moe_kernel, grid_spec=gs,
        out_shape=jax.ShapeDtypeStruct((ns, d), F32),
        compiler_params=_cparams(("arbitrary",)),
        name="moe_experts",
    )(meta, xs, w_gate, w_up, w_down)


def _moe_combine_kernel(s_ref, slot_ref, nslot_ref, x_ref, rt_ref, g2_ref, g_ref, sc_ref, sh_ref, ys_hbm,
                        x2_ref, h_ref, ybuf, sem, *, modulate):
    i = pl.program_id(0)
    n = pl.num_programs(0)
    te = x_ref.shape[0]

    def row_copy(idx_ref, buf, r, k):
        return pltpu.make_async_copy(ys_hbm.at[pl.ds(idx_ref[0, TOP_K * r + k], 1)],
                                     ybuf.at[buf, k, pl.ds(r, 1)], sem.at[buf])

    def start_tile(idx_ref, buf):
        def body(r, c):
            for k in range(TOP_K):
                row_copy(idx_ref, buf, r, k).start()
            return c
        lax.fori_loop(0, te, body, 0, unroll=ROW_UNROLL)

    cur = i % 2

    @pl.when(i == 0)
    def _():
        start_tile(slot_ref, 0)

    @pl.when(i + 1 < n)
    def _():
        start_tile(nslot_ref, 1 - cur)

    def wait_body(r, c):
        for k in range(TOP_K):
            row_copy(slot_ref, cur, r, k).wait()
        return c

    lax.fori_loop(0, te, wait_body, 0, unroll=ROW_UNROLL)
    w1 = rt_ref[:, TOP_K:TOP_K + 1]
    w2 = rt_ref[:, TOP_K + 1:TOP_K + 2]
    x2 = x_ref[...] + g2_ref[...] * (w1 * ybuf[cur, 0] + w2 * ybuf[cur, 1])
    x2_ref[...] = x2
    h = _rms(x2) * g_ref[...]
    if modulate:
        h = h * (1.0 + sc_ref[...]) + sh_ref[...]
    h_ref[...] = h.astype(h_ref.dtype)


def moe_combine(x1, ys, slot, route, g, mods_prev, mods_next, seg_ids, te, modulate, h_dtype):
    nt, d = x1.shape
    n = nt // te
    slot3 = slot.reshape(n, 1, TOP_K * te)
    gs = pltpu.PrefetchScalarGridSpec(
        num_scalar_prefetch=1, grid=(n,),
        in_specs=[pl.BlockSpec((None, 1, TOP_K * te), lambda i, s: (i, 0, 0), memory_space=pltpu.SMEM),
                  pl.BlockSpec((None, 1, TOP_K * te), lambda i, s: (jnp.minimum(i + 1, n - 1), 0, 0),
                               memory_space=pltpu.SMEM),
                  pl.BlockSpec((te, d), lambda i, s: (i, 0)),
                  pl.BlockSpec((te, LANES), lambda i, s: (i, 0)),
                  pl.BlockSpec((None, 1, d), lambda i, s: (s[i], 0, 5)),
                  pl.BlockSpec((1, d), lambda i, s: (0, 0)),
                  pl.BlockSpec((None, 1, d), lambda i, s: (s[i], 0, 1)),
                  pl.BlockSpec((None, 1, d), lambda i, s: (s[i], 0, 0)),
                  pl.BlockSpec(memory_space=pl.ANY)],
        out_specs=[pl.BlockSpec((te, d), lambda i, s: (i, 0)),
                   pl.BlockSpec((te, d), lambda i, s: (i, 0))],
        scratch_shapes=[pltpu.VMEM((2, TOP_K, te, d), F32), pltpu.SemaphoreType.DMA((2,))])
    return pl.pallas_call(
        functools.partial(_moe_combine_kernel, modulate=modulate), grid_spec=gs,
        out_shape=[jax.ShapeDtypeStruct((nt, d), F32), jax.ShapeDtypeStruct((nt, d), h_dtype)],
        compiler_params=_cparams(("arbitrary",)),
        name="moe_combine",
    )(seg_ids, slot3, slot3, x1, route, mods_prev, g.reshape(1, d), mods_next, mods_next, ys)


def kernel(x_prompt, x_sample, c, cache_na_k, cache_na_v, cache_gqa_k, cache_gqa_v, c_ctx, norm1_g, norm2_g, final_norm_g, ada_w, ada_b, w_in, w_out, na_rpb, q_norm_g, k_norm_g, hy_conv_w, hy_conv_b, hy_w1, hy_b1, hy_w2, hy_b2, hy_w3, hy_b3, hy_freq, hy_w_out, hy_bias, router_group_w, router_group_b, router_expert_w, router_expert_b, moe_w_gate, moe_w_up, moe_w_down):
    bc, lc, d = x_prompt.shape
    bl, ll, _ = x_sample.shape
    depth = w_in.shape[0]
    past = cache_na_k.shape[2]
    dh = HEAD_DIM
    na_w, gq_w = d // 4, d // 2
    hy_w = d - na_w - gq_w
    na_heads = na_w // dh
    nc = bc * lc
    nt = nc + bl * ll
    tm = 512
    moe_tm = 256
    assert nc % tm == 0 and ll % tm == 0 and past == lc

    te = 256

    def seg_of_tiles(t):
        return jnp.asarray(np.concatenate(
            [np.zeros(nc // t), 1 + np.repeat(np.arange(bl), ll // t)]).astype(np.int32))

    seg_e = seg_of_tiles(te)
    nseg = 1 + bl
    cond = jnp.zeros((8, d), F32).at[0].set(c_ctx).at[1:1 + bl].set(c)
    mods_all = modulation_all(cond, ada_w, ada_b)

    ropes = rope_tables(ll)
    tabs_c = dft_tables(lc)
    tabs_l = dft_tables(ll)
    ck_na = cache_na_k.reshape(bl, depth, past, na_w)
    cv_na = cache_na_v.reshape(bl, depth, past, na_w)
    ck_gq = cache_gqa_k.reshape(bl, depth, past, GQA_KV_HEADS * dh)
    cv_gq = cache_gqa_v.reshape(bl, depth, past, GQA_KV_HEADS * dh)

    x = jnp.concatenate([x_prompt.reshape(nc, d), x_sample.reshape(bl * ll, d)], axis=0)
    mods = mods_all[0, :nseg].reshape(nseg, 1, 6 * d)
    h = norm_mod(x, norm1_g[0], mods, 1, 0, seg_e, te, BF16)
    st_na_k, st_na_v, st_gq_k, st_gq_v = [], [], [], []
    hy_col0 = 3 * na_w + gq_w + 2 * GQA_KV_HEADS * dh
    moe_x = jnp.zeros((nt * TOP_K + N_EXPERTS * moe_tm, d), F32)
    for l in range(depth):
        z = in_proj(h, w_in, l)
        zc = z[:nc]
        st_na_k.append(zc[:, na_w:2 * na_w].reshape(bc, lc, na_heads, dh))
        st_na_v.append(zc[:, 2 * na_w:3 * na_w].reshape(bc, lc, na_heads, dh))
        st_gq_v.append(zc[:, 3 * na_w + gq_w + GQA_KV_HEADS * dh:hy_col0].reshape(bc, lc, GQA_KV_HEADS, dh))

        ctx_a, ctx_b, bk = ctx_attn(z, q_norm_g[l], k_norm_g[l], bc, lc, na_w, gq_w)
        st_gq_k.append(bk.reshape(bc, lc, GQA_KV_HEADS, dh))

        bias = na_bias(na_rpb[l])
        lat_a = lat_na(z, ck_na, cv_na, l, bias, nc, bl, ll, na_w)
        q, k, v = lat_prep(z, ropes, q_norm_g[l], k_norm_g[l], nc, bl, ll, gq_w)
        lat_b = lat_gqa(q, k, v, ck_gq, cv_gq, l)

        mix_c = []
        for (row0, nb, L, tabs) in ((0, bc, lc, tabs_c), (nc, bl, ll, tabs_l)):
            ctab, s1tab, s2tab = tabs
            x0, zz, zb = hy_pre(z, hy_conv_w[l], hy_conv_b[l], row0, nb, L, hy_col0, hy_w)
            filt = hy_filter(L, hy_w1[l], hy_b1[l], hy_w2[l], hy_b2[l], hy_w3[l], hy_b3[l],
                             hy_freq[l], hy_w_out[l])
            kp, kq = dft_fwd(ctab, s1tab, filt)
            xr, xs = dft_fwd(ctab, s1tab, zb)
            yr, yi = spec_mul(xr, xs, kp, kq)
            mix_c.append(dft_inv(ctab, s2tab, yr, yi, x0, zz, hy_bias[l]))

        x1, h2, route = out_proj_router((ctx_a, ctx_b, mix_c[0]), (lat_a, lat_b, mix_c[1]),
                                        w_out[l].astype(BF16), x, norm2_g[l], mods, seg_e, te,
                                        router_group_w[l], router_group_b[l],
                                        router_expert_w[l], router_expert_b[l])
        meta, slot = moe_plan(route, moe_tm)
        moe_x = moe_dispatch(h2, slot, moe_x, te)
        ys = moe_experts(moe_x, meta, moe_w_gate, moe_w_up, moe_w_down, l, moe_tm)
        last = l == depth - 1
        mods_next = mods if last else mods_all[l + 1, :nseg].reshape(nseg, 1, 6 * d)
        g_next = final_norm_g if last else norm1_g[l + 1]
        x, h = moe_combine(x1, ys, slot, route, g_next, mods, mods_next, seg_e, te, not last,
                           F32 if last else BF16)
        mods = mods_next

    y_prompt = h[:nc].reshape(bc, lc, d)
    y_sample = h[nc:].reshape(bl, ll, d)
    return (y_prompt, y_sample, jnp.stack(st_na_k, axis=1), jnp.stack(st_na_v, axis=1),
            jnp.stack(st_gq_k, axis=1), jnp.stack(st_gq_v, axis=1))
```

```python
import functools
import math

import numpy as np
import jax
import jax.numpy as jnp
from jax import lax
from jax.experimental import pallas as pl
from jax.experimental.pallas import tpu as pltpu

F32 = jnp.float32
BF16 = jnp.bfloat16
HIGHEST = lax.Precision.HIGHEST

GRID_W = 64
HEAD_DIM = 128
NA_WIN_R = 8
NA_WIN_C = 16
GQA_KV_HEADS = 2
ROPE_THETA = 10000.0
HY_EMB = 33
HY_HID = 64
HY_FAST_DECAY = 0.3
HY_SLOW_DECAY = 1.5
HY_TARGET = 0.01
N_GROUPS = 4
EXP_PER_GROUP = 4
N_EXPERTS = N_GROUPS * EXP_PER_GROUP
TOP_K = 2
EPS = 1e-6
NEG_INF = -1e30

LANES = 128
VMEM_LIMIT = 52 * 1024 * 1024


def _cparams(sem, vmem=VMEM_LIMIT):
    return pltpu.CompilerParams(dimension_semantics=sem, vmem_limit_bytes=vmem)


def _dot(a, b):
    return jnp.dot(a, b, preferred_element_type=F32)


def _dot_t(a, b):
    return lax.dot_general(a, b, (((1,), (1,)), ((), ())), preferred_element_type=F32)


def _dot_hi(a, b):
    return jnp.dot(a, b, preferred_element_type=F32, precision=HIGHEST)


def _mod_kernel(c_ref, w_ref, b_ref, o_ref):
    c = c_ref[...]
    s = c * (1.0 / (1.0 + jnp.exp(-c)))
    o_ref[...] = _dot_hi(s, w_ref[...]) + b_ref[...]


def modulation_all(cond, ada_w, ada_b, tn=1536):
    depth, d, n = ada_w.shape
    r = cond.shape[0]
    return pl.pallas_call(
        _mod_kernel,
        grid=(depth, n // tn),
        in_specs=[pl.BlockSpec((r, d), lambda l, j: (0, 0)),
                  pl.BlockSpec((None, d, tn), lambda l, j: (l, 0, j)),
                  pl.BlockSpec((None, 1, tn), lambda l, j: (l, 0, j))],
        out_specs=pl.BlockSpec((None, r, tn), lambda l, j: (l, 0, j)),
        out_shape=jax.ShapeDtypeStruct((depth, r, n), F32),
        compiler_params=_cparams(("arbitrary", "arbitrary")),
        name="modulation",
    )(cond, ada_w, ada_b.reshape(depth, 1, n))


def _rms(x):
    return x * lax.rsqrt(jnp.mean(x * x, axis=-1, keepdims=True) + EPS)


def _norm_mod_kernel(s_ref, x_ref, g_ref, sc_ref, sh_ref, o_ref):
    h = _rms(x_ref[...]) * g_ref[...]
    o_ref[...] = (h * (1.0 + sc_ref[...]) + sh_ref[...]).astype(o_ref.dtype)


def norm_mod(x, g, mods, sc_idx, sh_idx, seg_ids, tm, out_dtype):
    nt, d = x.shape
    gs = pltpu.PrefetchScalarGridSpec(
        num_scalar_prefetch=1, grid=(nt // tm,),
        in_specs=[pl.BlockSpec((tm, d), lambda i, s: (i, 0)),
                  pl.BlockSpec((1, d), lambda i, s: (0, 0)),
                  pl.BlockSpec((None, 1, d), lambda i, s: (s[i], 0, sc_idx)),
                  pl.BlockSpec((None, 1, d), lambda i, s: (s[i], 0, sh_idx))],
        out_specs=pl.BlockSpec((tm, d), lambda i, s: (i, 0)))
    return pl.pallas_call(
        _norm_mod_kernel, grid_spec=gs,
        out_shape=jax.ShapeDtypeStruct((nt, d), out_dtype),
        compiler_params=_cparams(("arbitrary",)),
        name="norm_mod",
    )(seg_ids, x, g.reshape(1, d), mods, mods)


def _mm_kernel(a_ref, w_ref, o_ref, wbf_ref):
    @pl.when(pl.program_id(1) == 0)
    def _():
        wbf_ref[...] = w_ref[...].astype(BF16)

    o_ref[...] = _dot(a_ref[...], wbf_ref[...])


def in_proj(h, w_in, layer, tm=1024, tn=768):
    nt, d = h.shape
    tm = math.gcd(tm, nt)
    n = w_in.shape[2]
    return pl.pallas_call(
        _mm_kernel,
        grid=(n // tn, nt // tm),
        in_specs=[pl.BlockSpec((tm, d), lambda j, i: (i, 0)),
                  pl.BlockSpec((None, d, tn), lambda j, i: (layer, 0, j))],
        out_specs=pl.BlockSpec((tm, tn), lambda j, i: (i, j)),
        out_shape=jax.ShapeDtypeStruct((nt, n), F32),
        scratch_shapes=[pltpu.VMEM((d, tn), BF16)],
        compiler_params=_cparams(("arbitrary", "arbitrary")),
        name="in_proj",
    )(h, w_in)


def _softmax_pv(s, v):
    m = jnp.max(s, axis=-1, keepdims=True)
    p = jnp.exp(s - m)
    l = jnp.sum(p, axis=-1, keepdims=True)
    return _dot(p.astype(BF16), v) / l


def _ctx_attn_kernel(za_ref, zb_ref, qg_ref, kg_ref, a_ref, b_ref, bk_ref, *, na_heads, gq_heads):
    dh = HEAD_DIM
    scale = dh ** -0.5
    L = za_ref.shape[0]
    na_w = na_heads * dh
    for h in range(na_heads):
        q = za_ref[:, h * dh:(h + 1) * dh].astype(BF16)
        k = za_ref[:, na_w + h * dh:na_w + (h + 1) * dh].astype(BF16)
        v = za_ref[:, 2 * na_w + h * dh:2 * na_w + (h + 1) * dh].astype(BF16)
        s = _dot_t(q, k) * scale
        a_ref[:, h * dh:(h + 1) * dh] = _softmax_pv(s, v).astype(a_ref.dtype)
    gq_w = gq_heads * dh
    group = gq_heads // GQA_KV_HEADS
    for kv in range(GQA_KV_HEADS):
        kn = _rms(zb_ref[:, gq_w + kv * dh:gq_w + (kv + 1) * dh]) * kg_ref[...]
        bk_ref[:, kv * dh:(kv + 1) * dh] = kn
        knb = kn.astype(BF16)
        v = zb_ref[:, gq_w + (GQA_KV_HEADS + kv) * dh:gq_w + (GQA_KV_HEADS + kv + 1) * dh].astype(BF16)
        for g in range(group):
            h = kv * group + g
            qn = (_rms(zb_ref[:, h * dh:(h + 1) * dh]) * qg_ref[...]).astype(BF16)
            s = _dot_t(qn, knb) * scale
            b_ref[:, h * dh:(h + 1) * dh] = _softmax_pv(s, v).astype(b_ref.dtype)


def ctx_attn(z, q_g, k_g, bc, lc, na_w, gq_w):
    dh = HEAD_DIM
    wa = 3 * na_w
    wb = gq_w + 2 * GQA_KV_HEADS * dh
    assert wa == wb
    kern = functools.partial(_ctx_attn_kernel, na_heads=na_w // dh, gq_heads=gq_w // dh)
    return pl.pallas_call(
        kern, grid=(bc,),
        in_specs=[pl.BlockSpec((lc, wa), lambda b: (b, 0)),
                  pl.BlockSpec((lc, wb), lambda b: (b, 1)),
                  pl.BlockSpec((1, dh), lambda b: (0, 0)),
                  pl.BlockSpec((1, dh), lambda b: (0, 0))],
        out_specs=[pl.BlockSpec((lc, na_w), lambda b: (b, 0)),
                   pl.BlockSpec((lc, gq_w), lambda b: (b, 0)),
                   pl.BlockSpec((lc, GQA_KV_HEADS * dh), lambda b: (b, 0))],
        out_shape=[jax.ShapeDtypeStruct((bc * lc, na_w), BF16),
                   jax.ShapeDtypeStruct((bc * lc, gq_w), BF16),
                   jax.ShapeDtypeStruct((bc * lc, GQA_KV_HEADS * dh), F32)],
        compiler_params=_cparams(("arbitrary",)),
        name="ctx_attn",
    )(z, z, q_g.reshape(1, dh), k_g.reshape(1, dh))


def _rope_kernel(cos_ref, sa_ref, sb_ref):
    n, dh = cos_ref.shape
    quarter = dh // 2
    t = lax.broadcasted_iota(jnp.int32, (n, dh), 0)
    lane = lax.broadcasted_iota(jnp.int32, (n, dh), 1)
    f = (lane & (quarter // 2 - 1)).astype(F32)
    inv = jnp.exp(-(2.0 * f / quarter) * math.log(ROPE_THETA))
    pos = jnp.where(lane < quarter, t >> int(math.log2(GRID_W)), t & (GRID_W - 1)).astype(F32)
    ang = pos * inv
    sin = jnp.sin(ang)
    first = (lane & (quarter - 1)) < (quarter // 2)
    cos_ref[...] = jnp.cos(ang)
    sa_ref[...] = jnp.where(first, -sin, 0.0)
    sb_ref[...] = jnp.where(first, 0.0, sin)


def rope_tables(n):
    shp = jax.ShapeDtypeStruct((n, HEAD_DIM), F32)
    return pl.pallas_call(_rope_kernel, out_shape=[shp, shp, shp], name="rope_tables")()


def _rope(x, cos, sa, sb):
    q4 = HEAD_DIM // 4
    return x * cos + pltpu.roll(x, HEAD_DIM - q4, 1) * sa + pltpu.roll(x, q4, 1) * sb


def _lat_prep_kernel(z_ref, cos_ref, sa_ref, sb_ref, qg_ref, kg_ref, q_ref, k_ref, v_ref, *, gq_heads):
    dh = HEAD_DIM
    scale = dh ** -0.5
    cos, sa, sb = cos_ref[...], sa_ref[...], sb_ref[...]
    for h in range(gq_heads):
        qn = _rms(z_ref[:, h * dh:(h + 1) * dh]) * qg_ref[...]
        q_ref[h] = (_rope(qn, cos, sa, sb) * scale).astype(BF16)
    gq_w = gq_heads * dh
    for kv in range(GQA_KV_HEADS):
        kn = _rms(z_ref[:, gq_w + kv * dh:gq_w + (kv + 1) * dh]) * kg_ref[...]
        k_ref[kv] = _rope(kn, cos, sa, sb).astype(BF16)
        v_ref[kv] = z_ref[:, gq_w + (GQA_KV_HEADS + kv) * dh:gq_w + (GQA_KV_HEADS + kv + 1) * dh].astype(BF16)


def lat_prep(z, ropes, q_g, k_g, nc, bl, ll, gq_w, tl=512):
    dh = HEAD_DIM
    gq_heads = gq_w // dh
    wb = gq_w + 2 * GQA_KV_HEADS * dh
    nb = ll // tl
    off = nc // tl
    cos, sa, sb = ropes
    tab = pl.BlockSpec((tl, dh), lambda b, i: (i, 0))
    one = pl.BlockSpec((1, dh), lambda b, i: (0, 0))
    return pl.pallas_call(
        functools.partial(_lat_prep_kernel, gq_heads=gq_heads),
        grid=(bl, nb),
        in_specs=[pl.BlockSpec((tl, wb), lambda b, i: (off + b * nb + i, 1)), tab, tab, tab, one, one],
        out_specs=[pl.BlockSpec((None, gq_heads, tl, dh), lambda b, i: (b, 0, i, 0)),
                   pl.BlockSpec((None, GQA_KV_HEADS, tl, dh), lambda b, i: (b, 0, i, 0)),
                   pl.BlockSpec((None, GQA_KV_HEADS, tl, dh), lambda b, i: (b, 0, i, 0))],
        out_shape=[jax.ShapeDtypeStruct((bl, gq_heads, ll, dh), BF16),
                   jax.ShapeDtypeStruct((bl, GQA_KV_HEADS, ll, dh), BF16),
                   jax.ShapeDtypeStruct((bl, GQA_KV_HEADS, ll, dh), BF16)],
        compiler_params=_cparams(("arbitrary", "arbitrary")),
        name="lat_prep",
    )(z, cos, sa, sb, q_g.reshape(1, dh), k_g.reshape(1, dh))


def _lat_gqa_kernel(q_ref, k_ref, v_ref, ck_ref, cv_ref, o_ref, *, tk):
    group, tq, dh = q_ref.shape
    rows = group * tq
    q = q_ref[...].reshape(rows, dh)
    ll = k_ref.shape[0]

    def update(carry, kc, vc):
        m, l, acc = carry
        s = _dot_t(q, kc)
        m_new = jnp.maximum(m, jnp.max(s, axis=-1, keepdims=True))
        alpha = jnp.exp(m - m_new)
        p = jnp.exp(s - m_new)
        l = alpha * l + jnp.sum(p, axis=-1, keepdims=True)
        acc = alpha * acc + _dot(p.astype(BF16), vc)
        return m_new, l, acc

    def body(c, carry):
        st = pl.multiple_of(c * tk, tk)
        return update(carry, k_ref[pl.ds(st, tk), :], v_ref[pl.ds(st, tk), :])

    init = (jnp.full((rows, 1), -jnp.inf, F32), jnp.zeros((rows, 1), F32), jnp.zeros((rows, dh), F32))
    carry = lax.fori_loop(0, ll // tk, body, init)
    m, l, acc = update(carry, ck_ref[...].astype(BF16), cv_ref[...].astype(BF16))
    out = acc / l
    for g in range(group):
        o_ref[:, g * dh:(g + 1) * dh] = out[g * tq:(g + 1) * tq].astype(o_ref.dtype)


def lat_gqa(q, k, v, cache_k, cache_v, layer, tq=256, tk=4096):
    bl, gq_heads, ll, dh = q.shape
    group = gq_heads // GQA_KV_HEADS
    past = cache_k.shape[2]
    nq = ll // tq
    tk = min(tk, ll)
    assert ll % tk == 0
    ck = pl.BlockSpec((None, None, past, dh), lambda b, h, i: (b, layer, 0, h))
    return pl.pallas_call(
        functools.partial(_lat_gqa_kernel, tk=tk),
        grid=(bl, GQA_KV_HEADS, nq),
        in_specs=[pl.BlockSpec((None, group, tq, dh), lambda b, h, i: (b, h, i, 0)),
                  pl.BlockSpec((None, None, ll, dh), lambda b, h, i: (b, h, 0, 0)),
                  pl.BlockSpec((None, None, ll, dh), lambda b, h, i: (b, h, 0, 0)),
                  ck, ck],
        out_specs=pl.BlockSpec((tq, group * dh), lambda b, h, i: (b * nq + i, h)),
        out_shape=jax.ShapeDtypeStruct((bl * ll, gq_heads * dh), BF16),
        compiler_params=_cparams(("arbitrary", "arbitrary", "arbitrary")),
        name="lat_gqa",
    )(q, k, v, cache_k, cache_v)


def _na_bias_kernel(rpb_ref, o_ref):
    h = pl.program_id(0)
    w = GRID_W
    qc = lax.broadcasted_iota(jnp.int32, (w, w), 0)
    kc = lax.broadcasted_iota(jnp.int32, (w, w), 1)
    ci = jnp.clip(kc - qc + NA_WIN_C - 1, 0, 2 * NA_WIN_C - 2)
    cs = jnp.clip(qc - NA_WIN_C // 2, 0, w - NA_WIN_C)
    col_ok = (kc >= cs) & (kc < cs + NA_WIN_C)
    nr, ncol = 2 * NA_WIN_R - 1, 2 * NA_WIN_C - 1
    tiles = []
    for dr in range(nr):
        t = jnp.zeros((w, w), F32)
        for c in range(ncol):
            t = jnp.where(ci == c, rpb_ref[h, dr * ncol + c], t)
        tiles.append(jnp.where(col_ok, t, NEG_INF))
    for d0 in range(NA_WIN_R):
        for j in range(NA_WIN_R):
            o_ref[d0, :, j * w:(j + 1) * w] = tiles[d0 + j]


def na_bias(rpb_l):
    heads = rpb_l.shape[0]
    flat = rpb_l.reshape(heads, -1)
    return pl.pallas_call(
        _na_bias_kernel, grid=(heads,),
        in_specs=[pl.BlockSpec(memory_space=pltpu.SMEM)],
        out_specs=pl.BlockSpec((None, NA_WIN_R, GRID_W, NA_WIN_R * GRID_W), lambda h: (h, 0, 0, 0)),
        out_shape=jax.ShapeDtypeStruct((heads, NA_WIN_R, GRID_W, NA_WIN_R * GRID_W), F32),
        compiler_params=_cparams(("arbitrary",)),
        name="na_bias",
    )(flat)


def _lat_na_kernel(q_ref, k_ref, v_ref, ck_ref, cv_ref, bias_ref, o_ref, kb_ref, vb_ref, *, rb, rows):
    w = GRID_W
    scale = HEAD_DIM ** -0.5
    i = pl.program_id(2)

    @pl.when(i == 0)
    def _():
        kb_ref[...] = k_ref[...].astype(BF16)
        vb_ref[...] = v_ref[...].astype(BF16)

    ck = ck_ref[...].astype(BF16)
    cv = cv_ref[...].astype(BF16)
    qs = (q_ref[...] * scale).astype(BF16)
    s2 = _dot_t(qs, ck)
    s1_rows, v_wins = [], []
    for j in range(rb):
        r = i * rb + j
        rs = jnp.clip(r - NA_WIN_R // 2, 0, rows - NA_WIN_R)
        d0 = rs - r + NA_WIN_R - 1
        st = pl.multiple_of(rs * w, w)
        kw = kb_ref[pl.ds(st, NA_WIN_R * w), :]
        v_wins.append(vb_ref[pl.ds(st, NA_WIN_R * w), :])
        s1_rows.append(_dot_t(qs[j * w:(j + 1) * w], kw) + bias_ref[d0])
    s1 = jnp.concatenate(s1_rows, axis=0)
    m = jnp.maximum(jnp.max(s1, axis=-1, keepdims=True), jnp.max(s2, axis=-1, keepdims=True))
    p1 = jnp.exp(s1 - m)
    p2 = jnp.exp(s2 - m)
    l = jnp.sum(p1, axis=-1, keepdims=True) + jnp.sum(p2, axis=-1, keepdims=True)
    p1 = p1.astype(BF16)
    o2 = _dot(p2.astype(BF16), cv)
    o1 = jnp.concatenate([_dot(p1[j * w:(j + 1) * w], v_wins[j]) for j in range(rb)], axis=0)
    o_ref[...] = ((o1 + o2) / l).astype(o_ref.dtype)


def lat_na(z, cache_k, cache_v, layer, bias, nc, bl, ll, na_w, rb=32):
    dh = HEAD_DIM
    heads = na_w // dh
    rows = ll // GRID_W
    rb = min(rb, rows)
    assert rows >= NA_WIN_R and nc % ll == 0 and rows % rb == 0
    past = cache_k.shape[2]
    nb = rows // rb
    tq = rb * GRID_W
    off = nc // tq
    lat0 = nc // ll
    ck = pl.BlockSpec((None, None, past, dh), lambda b, h, i: (b, layer, 0, h))
    return pl.pallas_call(
        functools.partial(_lat_na_kernel, rb=rb, rows=rows),
        grid=(bl, heads, nb),
        in_specs=[pl.BlockSpec((tq, dh), lambda b, h, i: (off + b * nb + i, h)),
                  pl.BlockSpec((ll, dh), lambda b, h, i: (lat0 + b, heads + h)),
                  pl.BlockSpec((ll, dh), lambda b, h, i: (lat0 + b, 2 * heads + h)),
                  ck, ck,
                  pl.BlockSpec((None, NA_WIN_R, GRID_W, NA_WIN_R * GRID_W), lambda b, h, i: (h, 0, 0, 0))],
        out_specs=pl.BlockSpec((tq, dh), lambda b, h, i: (b * nb + i, h)),
        out_shape=jax.ShapeDtypeStruct((bl * ll, na_w), BF16),
        scratch_shapes=[pltpu.VMEM((ll, dh), BF16), pltpu.VMEM((ll, dh), BF16)],
        compiler_params=_cparams(("arbitrary", "arbitrary", "arbitrary")),
        name="lat_na",
    )(z, z, z, cache_k, cache_v, bias)


def _hy_pre_kernel(u0_ref, u1_ref, u2_ref, w0_ref, w1_ref, w2_ref, b0_ref, b1_ref, b2_ref,
                   x0_ref, zz_ref, zb_ref):
    L = u0_ref.shape[0]
    t = lax.broadcasted_iota(jnp.int32, u0_ref.shape, 0)

    def conv(u_ref, w_ref, b_ref):
        u = u_ref[...]
        prev = jnp.where(t == 0, 0.0, pltpu.roll(u, 1, 0))
        nxt = jnp.where(t == L - 1, 0.0, pltpu.roll(u, L - 1, 0))
        return prev * w_ref[0:1, :] + u * w_ref[1:2, :] + nxt * w_ref[2:3, :] + b_ref[...]

    x0_ref[...] = conv(u0_ref, w0_ref, b0_ref)
    zz = conv(u2_ref, w2_ref, b2_ref) * conv(u1_ref, w1_ref, b1_ref)
    zz_ref[...] = zz
    zb_ref[...] = zz.astype(BF16)


def hy_pre(z, conv_w, conv_b, row0, nb, L, col0, hy_w):
    cb = LANES
    nj = hy_w // cb
    c0 = col0 // cb
    r0 = row0 // L
    assert row0 % L == 0
    u = lambda part: pl.BlockSpec((L, cb), lambda b, j: (r0 + b, c0 + part * nj + j))
    wspec = lambda part: pl.BlockSpec((3, cb), lambda b, j: (0, part * nj + j))
    bspec = lambda part: pl.BlockSpec((1, cb), lambda b, j: (0, part * nj + j))
    o = pl.BlockSpec((None, L, cb), lambda b, j: (b, 0, j))
    return pl.pallas_call(
        _hy_pre_kernel, grid=(nb, nj),
        in_specs=[u(0), u(1), u(2), wspec(0), wspec(1), wspec(2), bspec(0), bspec(1), bspec(2)],
        out_specs=[o, o, o],
        out_shape=[jax.ShapeDtypeStruct((nb, L, hy_w), F32), jax.ShapeDtypeStruct((nb, L, hy_w), F32),
                   jax.ShapeDtypeStruct((nb, L, hy_w), BF16)],
        compiler_params=_cparams(("arbitrary", "arbitrary")),
        name="hy_pre",
    )(z, z, z, conv_w, conv_w, conv_w, conv_b.reshape(1, -1), conv_b.reshape(1, -1), conv_b.reshape(1, -1))


def _hy_filter_kernel(w1_ref, b1_ref, w2_ref, b2_ref, w3_ref, b3_ref, fq_ref, wo_ref, o_ref, *, hy_w):
    L = o_ref.shape[1]
    bands = (HY_EMB - 1) // 2
    ti = lax.broadcasted_iota(jnp.int32, (L, LANES), 0).astype(F32)
    lane = lax.broadcasted_iota(jnp.int32, (L, LANES), 1)
    t01 = ti / (L - 1)
    w = (2.0 * math.pi / L) * ti
    band = ((lane - 1) % bands).astype(F32)
    fr = 1e-4 + band * ((bands - 1 - 1e-4) / (bands - 1))
    ang = fr * w
    feat = jnp.where(lane == 0, t01,
                     jnp.where(lane <= bands, jnp.cos(ang),
                               jnp.where(lane <= 2 * bands, -jnp.sin(ang), 0.0)))
    fq = fq_ref[...]
    hdn = jnp.sin(fq * (_dot_hi(feat, w1_ref[...]) + b1_ref[...]))
    hdn = jnp.sin(fq * (_dot_hi(hdn, w2_ref[...]) + b2_ref[...]))
    hdn = jnp.sin(fq * (_dot_hi(hdn, w3_ref[...]) + b3_ref[...]))
    filt = _dot_hi(hdn, wo_ref[...])
    max_decay = math.log(HY_TARGET) / HY_FAST_DECAY
    min_decay = math.log(HY_TARGET) / HY_SLOW_DECAY
    ch = lax.broadcasted_iota(jnp.int32, (L, hy_w), 1).astype(F32)
    deltas = jnp.abs(min_decay + ch * ((max_decay - min_decay) / (hy_w - 1)))
    tt = lax.broadcasted_iota(jnp.int32, (L, hy_w), 0)
    decay = jnp.exp(-(tt.astype(F32) / (L - 1)) * deltas)
    o_ref[0] = (filt[:, :hy_w] * decay).astype(o_ref.dtype)
    o_ref[1] = jnp.where(tt == 0, 0.0, filt[:, hy_w:] * decay).astype(o_ref.dtype)


def _pad2(a, r, c):
    return jnp.pad(a, ((0, r - a.shape[0]), (0, c - a.shape[1])))


def hy_filter(L, w1, b1, w2, b2, w3, b3, freq, w_out):
    hy_w = w_out.shape[1] // 2
    p = LANES
    args = (_pad2(w1, p, p), _pad2(b1[None], 1, p), _pad2(w2, p, p), _pad2(b2[None], 1, p),
            _pad2(w3, p, p), _pad2(b3[None], 1, p), _pad2(freq[None], 1, p), _pad2(w_out, p, 2 * hy_w))
    return pl.pallas_call(
        functools.partial(_hy_filter_kernel, hy_w=hy_w),
        out_shape=jax.ShapeDtypeStruct((2, L, hy_w), BF16),
        compiler_params=_cparams(None),
        name="hy_filter",
    )(*args)


def _dft_tables_kernel(c_ref, s1_ref, s2_ref, *, L):
    tk = c_ref.shape[0]
    n2 = 2 * L
    theta = 2.0 * math.pi / n2
    k = pl.program_id(0) * tk + lax.broadcasted_iota(jnp.int32, (tk, LANES), 0)
    lane = lax.broadcasted_iota(jnp.int32, (tk, LANES), 1)
    a = ((k * lane) & (n2 - 1)).astype(F32) * theta
    b = ((k * LANES * lane) & (n2 - 1)).astype(F32) * theta
    ca, sa, cb, sb = jnp.cos(a), jnp.sin(a), jnp.cos(b), jnp.sin(b)
    sign_k = (1 - 2 * (k & 1)).astype(F32)
    for j in range(L // LANES):
        cbj = cb[:, j:j + 1]
        sbj = sb[:, j:j + 1]
        c = ca * cbj - sa * sbj
        s = sa * cbj + ca * sbj
        t = lane + j * LANES
        sign_t = (1 - 2 * (t & 1)).astype(F32)
        c_ref[:, j * LANES:(j + 1) * LANES] = c.astype(BF16)
        s1_ref[:, j * LANES:(j + 1) * LANES] = jnp.where(k == 0, sign_t, s).astype(BF16)
        s2_ref[:, j * LANES:(j + 1) * LANES] = jnp.where(t == 0, sign_k, s).astype(BF16)


def dft_tables(L, tk=256):
    tk = min(tk, L)
    shp = jax.ShapeDtypeStruct((L, L), BF16)
    spec = pl.BlockSpec((tk, L), lambda i: (i, 0))
    return pl.pallas_call(
        functools.partial(_dft_tables_kernel, L=L), grid=(L // tk,),
        out_specs=[spec, spec, spec], out_shape=[shp, shp, shp],
        compiler_params=_cparams(("arbitrary",)),
        name="dft_tables",
    )()


def _dft_fwd_kernel(c_ref, s_ref, z_ref, re_ref, im_ref):
    z = z_ref[...]
    re_ref[...] = _dot(c_ref[...], z)
    im_ref[...] = _dot(s_ref[...], z)


def dft_fwd(ctab, stab, zb, tm=512):
    nb, L, cw = zb.shape
    tm = min(tm, L)
    tab = pl.BlockSpec((tm, L), lambda i, b: (i, 0))
    o = pl.BlockSpec((None, tm, cw), lambda i, b: (b, i, 0))
    shp = jax.ShapeDtypeStruct((nb, L, cw), F32)
    return pl.pallas_call(
        _dft_fwd_kernel, grid=(L // tm, nb),
        in_specs=[tab, tab, pl.BlockSpec((None, L, cw), lambda i, b: (b, 0, 0))],
        out_specs=[o, o], out_shape=[shp, shp],
        compiler_params=_cparams(("arbitrary", "arbitrary")),
        name="dft_fwd",
    )(ctab, stab, zb)


def _dft_mul_kernel(c_ref, s_ref, z_ref, kp_ref, kq_ref, yr_ref, yi_ref, *, n2):
    tl = c_ref.shape[0]
    z = z_ref[...]
    xr = _dot(c_ref[...], z)
    xs = _dot(s_ref[...], z)
    k = pl.program_id(0) * tl + lax.broadcasted_iota(jnp.int32, xr.shape, 0)
    kre = kp_ref[0] + kp_ref[1]
    kim = kq_ref[1] - kq_ref[0]
    knyq = kq_ref[0] + kq_ref[1]
    first = k == 0
    yre = xr * kre + jnp.where(first, 0.0, xs * kim)
    yim = xr * kim - xs * kre
    yr_ref[...] = (jnp.where(first, 1.0 / n2, 2.0 / n2) * yre).astype(BF16)
    yi_ref[...] = jnp.where(first, xs * knyq * (1.0 / n2), (-2.0 / n2) * yim).astype(BF16)


def dft_mul(ctab, stab, zb, kp, kq, tm=512):
    nb, L, cw = zb.shape
    tm = min(tm, L)
    tab = pl.BlockSpec((tm, L), lambda i, b: (i, 0))
    f = pl.BlockSpec((2, tm, cw), lambda i, b: (0, i, 0))
    o = pl.BlockSpec((None, tm, cw), lambda i, b: (b, i, 0))
    shp = jax.ShapeDtypeStruct((nb, L, cw), BF16)
    return pl.pallas_call(
        functools.partial(_dft_mul_kernel, n2=2 * L), grid=(L // tm, nb),
        in_specs=[tab, tab, pl.BlockSpec((None, L, cw), lambda i, b: (b, 0, 0)), f, f],
        out_specs=[o, o], out_shape=[shp, shp],
        compiler_params=_cparams(("arbitrary", "arbitrary")),
        name="dft_mul",
    )(ctab, stab, zb, kp, kq)


def _dft_inv_kernel(c_ref, s_ref, yr_ref, yi_ref, x0_ref, zz_ref, bd_ref, o_ref):
    y = _dot(c_ref[...], yr_ref[...]) + _dot(s_ref[...], yi_ref[...])
    o_ref[...] = (x0_ref[...] * (y + zz_ref[...] * bd_ref[...])).astype(o_ref.dtype)


def dft_inv(ctab, s2tab, yr, yi, x0, zz, bias_d, tm=512):
    nb, L, cw = yr.shape
    tm = min(tm, L)
    nti = L // tm
    tab = pl.BlockSpec((tm, L), lambda i, b: (i, 0))
    full = pl.BlockSpec((None, L, cw), lambda i, b: (b, 0, 0))
    tile = pl.BlockSpec((None, tm, cw), lambda i, b: (b, i, 0))
    return pl.pallas_call(
        _dft_inv_kernel, grid=(nti, nb),
        in_specs=[tab, tab, full, full, tile, tile, pl.BlockSpec((1, cw), lambda i, b: (0, 0))],
        out_specs=pl.BlockSpec((tm, cw), lambda i, b: (b * nti + i, 0)),
        out_shape=jax.ShapeDtypeStruct((nb * L, cw), BF16),
        compiler_params=_cparams(("arbitrary", "arbitrary")),
        name="dft_inv",
    )(ctab, s2tab, yr, yi, x0, zz, bias_d.reshape(1, cw))


def _split_bf16(a):
    hi = a.astype(BF16)
    return hi, (a - hi.astype(F32)).astype(BF16)


def _dot_3pass(a, b):
    a_hi, a_lo = _split_bf16(a)
    b_hi, b_lo = _split_bf16(b)
    return _dot(a_hi, b_hi) + (_dot(a_lo, b_hi) + _dot(a_hi, b_lo))


def _route(logits):
    li = lax.broadcasted_iota(jnp.int32, logits.shape, 1)
    big = jnp.int32(LANES)
    ninf = -jnp.inf

    def rmax(x):
        return jnp.max(x, axis=-1, keepdims=True)

    def first_at(x, m):
        return jnp.min(jnp.where(x == m, li, big), axis=-1, keepdims=True)

    gmask = li < N_GROUPS
    gl = jnp.where(gmask, logits, ninf)
    gmax = rmax(gl)
    g_sel = first_at(gl, gmax)
    g_prob = 1.0 / jnp.sum(jnp.where(gmask, jnp.exp(gl - gmax), 0.0), axis=-1, keepdims=True)
    e = li - N_GROUPS
    emask = (e >= 0) & (e < N_EXPERTS) & ((e >> 2) == g_sel)
    el = jnp.where(emask, logits, ninf)
    emax = rmax(el)
    esum = jnp.sum(jnp.where(emask, jnp.exp(el - emax), 0.0), axis=-1, keepdims=True)
    l1 = first_at(el, emax)
    el2 = jnp.where(li == l1, ninf, el)
    emax2 = rmax(el2)
    l2 = first_at(el2, emax2)
    p1 = 1.0 / esum
    p2 = jnp.exp(emax2 - emax) / esum
    tot = p1 + p2
    w1 = p1 / tot * g_prob
    w2 = p2 / tot * g_prob
    return jnp.where(li == 0, (l1 - N_GROUPS).astype(F32),
                     jnp.where(li == 1, (l2 - N_GROUPS).astype(F32),
                               jnp.where(li == 2, w1, jnp.where(li == 3, w2, 0.0))))


def _out_router_kernel(s_ref, ca_ref, cb_ref, cc_ref, la_ref, lb_ref, lc_ref, w_ref, x_ref, g1_ref,
                       g_ref, sc_ref, sh_ref, wr_ref, br_ref, x1_ref, h_ref, rt_ref, *, nci):
    i = pl.program_id(0)

    def project(a_ref, b_ref, c_ref):
        wa = a_ref.shape[1]
        wb = b_ref.shape[1]
        acc = _dot(a_ref[...], w_ref[0:wa, :])
        acc += _dot(b_ref[...], w_ref[wa:wa + wb, :])
        acc += _dot(c_ref[...], w_ref[wa + wb:, :])
        x1_ref[...] = x_ref[...] + g1_ref[...] * acc

    @pl.when(i < nci)
    def _():
        project(ca_ref, cb_ref, cc_ref)

    @pl.when(i >= nci)
    def _():
        project(la_ref, lb_ref, lc_ref)

    h = _rms(x1_ref[...]) * g_ref[...]
    h = h * (1.0 + sc_ref[...]) + sh_ref[...]
    h_ref[...] = h
    rt_ref[...] = _route(_dot_3pass(h, wr_ref[...]) + br_ref[...])


def out_proj_router(mix_ctx, mix_lat, w_out_bf, x, g, mods, seg_ids, tm, wg, bg, we, be):
    nt, d = x.shape
    nci = mix_ctx[0].shape[0] // tm
    wr = _pad2(jnp.concatenate([wg, we], axis=1), d, LANES)
    br = _pad2(jnp.concatenate([bg, be])[None], 1, LANES)
    cspec = lambda a: pl.BlockSpec((tm, a.shape[1]), lambda i, s: (jnp.minimum(i, nci - 1), 0))
    lspec = lambda a: pl.BlockSpec((tm, a.shape[1]), lambda i, s: (jnp.maximum(i - nci, 0), 0))
    row = pl.BlockSpec((tm, d), lambda i, s: (i, 0))
    mod = lambda p: pl.BlockSpec((None, 1, d), lambda i, s: (s[i], 0, p))
    gs = pltpu.PrefetchScalarGridSpec(
        num_scalar_prefetch=1, grid=(nt // tm,),
        in_specs=[cspec(mix_ctx[0]), cspec(mix_ctx[1]), cspec(mix_ctx[2]),
                  lspec(mix_lat[0]), lspec(mix_lat[1]), lspec(mix_lat[2]),
                  pl.BlockSpec((d, d), lambda i, s: (0, 0)),
                  row, mod(2),
                  pl.BlockSpec((1, d), lambda i, s: (0, 0)), mod(4), mod(3),
                  pl.BlockSpec((d, LANES), lambda i, s: (0, 0)),
                  pl.BlockSpec((1, LANES), lambda i, s: (0, 0))],
        out_specs=[row, row, pl.BlockSpec((tm, LANES), lambda i, s: (i, 0))])
    return pl.pallas_call(
        functools.partial(_out_router_kernel, nci=nci), grid_spec=gs,
        out_shape=[jax.ShapeDtypeStruct((nt, d), F32), jax.ShapeDtypeStruct((nt, d), F32),
                   jax.ShapeDtypeStruct((nt, LANES), F32)],
        compiler_params=_cparams(("arbitrary",)),
        name="out_proj_router",
    )(seg_ids, *mix_ctx, *mix_lat, w_out_bf, x, mods, g.reshape(1, d), mods, mods, wr, br)


def moe_plan(route, tm):
    nt = route.shape[0]
    npair = nt * TOP_K
    ns = npair + N_EXPERTS * tm
    n_tiles = ns // tm
    e = route[:, :TOP_K].astype(jnp.int32).reshape(npair)
    onehot = (e[:, None] == jnp.arange(N_EXPERTS, dtype=jnp.int32)[None, :]).astype(jnp.int32)
    csum = jnp.cumsum(onehot, axis=0)
    counts = csum[-1]
    rank = jnp.sum((csum - onehot) * onehot, axis=1)
    padded = ((counts + tm - 1) // tm) * tm
    ends = jnp.cumsum(padded)
    starts = ends - padded
    slot = (jnp.sum(onehot * starts[None, :], axis=1) + rank).astype(jnp.int32)
    tile_start = jnp.arange(n_tiles, dtype=jnp.int32) * tm
    tile_e = jnp.sum((tile_start[:, None] >= ends[None, :]).astype(jnp.int32), axis=1)
    tile_valid = (tile_start < ends[-1]).astype(jnp.int32)
    last_e = jnp.max(jnp.where(counts > 0, jnp.arange(N_EXPERTS, dtype=jnp.int32), 0))
    tile_e = jnp.where(tile_valid > 0, tile_e, last_e).astype(jnp.int32)
    prev = jnp.concatenate([jnp.full((1,), -1, jnp.int32), tile_e[:-1]])
    tile_new = (tile_e != prev).astype(jnp.int32)
    meta = jnp.stack([tile_e, tile_valid, tile_new], axis=0)
    return meta, slot


ROW_UNROLL = 8


def _dispatch_kernel(slot_ref, h_ref, xs_in, xs_hbm, sem):
    del xs_in
    te = h_ref.shape[0]

    def row_copy(r, k):
        return pltpu.make_async_copy(h_ref.at[pl.ds(r, 1)], xs_hbm.at[pl.ds(slot_ref[0, TOP_K * r + k], 1)], sem)

    def start(r, c):
        for k in range(TOP_K):
            row_copy(r, k).start()
        return c

    lax.fori_loop(0, te, start, 0, unroll=ROW_UNROLL)

    def wait(r, c):
        for k in range(TOP_K):
            row_copy(r, k).wait()
        return c

    lax.fori_loop(0, te, wait, 0, unroll=ROW_UNROLL)


def moe_dispatch(h, slot, xs_prev, te):
    nt, d = h.shape
    n = nt // te
    ns = xs_prev.shape[0]
    return pl.pallas_call(
        _dispatch_kernel, grid=(n,),
        in_specs=[pl.BlockSpec((None, 1, TOP_K * te), lambda i: (i, 0, 0), memory_space=pltpu.SMEM),
                  pl.BlockSpec((te, d), lambda i: (i, 0)),
                  pl.BlockSpec(memory_space=pl.ANY)],
        out_specs=pl.BlockSpec(memory_space=pl.ANY),
        out_shape=jax.ShapeDtypeStruct((ns, d), F32),
        scratch_shapes=[pltpu.SemaphoreType.DMA],
        input_output_aliases={2: 0},
        compiler_params=_cparams(("arbitrary",)),
        name="moe_dispatch",
    )(slot.reshape(n, 1, TOP_K * te), h, xs_prev)


def _moe_kernel(meta_ref, x_ref, wg_ref, wu_ref, wd_ref, y_ref, wgb, wub, wdb):
    t = pl.program_id(0)

    @pl.when(meta_ref[1, t] > 0)
    def _():
        @pl.when(meta_ref[2, t] > 0)
        def _():
            wgb[...] = wg_ref[...].astype(BF16)
            wub[...] = wu_ref[...].astype(BF16)
            wdb[...] = wd_ref[...].astype(BF16)

        x = x_ref[...].astype(BF16)
        a = _dot(x, wgb[...])
        u = _dot(x, wub[...])
        hid = a * (1.0 / (1.0 + jnp.exp(-a))) * u
        y_ref[...] = _dot(hid.astype(BF16), wdb[...])

    @pl.when(meta_ref[1, t] == 0)
    def _():
        y_ref[...] = jnp.zeros_like(y_ref)


def moe_experts(xs, meta, w_gate, w_up, w_down, layer, tm):
    ns, d = xs.shape
    de = w_gate.shape[3]
    gs = pltpu.PrefetchScalarGridSpec(
        num_scalar_prefetch=1, grid=(ns // tm,),
        in_specs=[pl.BlockSpec((tm, d), lambda t, m: (t, 0)),
                  pl.BlockSpec((None, None, d, de), lambda t, m: (layer, m[0, t], 0, 0)),
                  pl.BlockSpec((None, None, d, de), lambda t, m: (layer, m[0, t], 0, 0)),
                  pl.BlockSpec((None, None, de, d), lambda t, m: (layer, m[0, t], 0, 0))],
        out_specs=pl.BlockSpec((tm, d), lambda t, m: (t, 0)),
        scratch_shapes=[pltpu.VMEM((d, de), BF16), pltpu.VMEM((d, de), BF16), pltpu.VMEM((de, d), BF16)])
    return pl.pallas_call(
        _moe_kernel, grid_spec=gs,
        out_shape=jax.ShapeDtypeStruct((ns, d), F32),
        compiler_params=_cparams(("arbitrary",)),
        name="moe_experts",
    )(meta, xs, w_gate, w_up, w_down)


def _moe_combine_kernel(s_ref, slot_ref, nslot_ref, x_ref, rt_ref, g2_ref, g_ref, sc_ref, sh_ref, ys_hbm,
                        x2_ref, h_ref, ybuf, sem, *, modulate):
    i = pl.program_id(0)
    n = pl.num_programs(0)
    te = x_ref.shape[0]

    def row_copy(idx_ref, buf, r, k):
        return pltpu.make_async_copy(ys_hbm.at[pl.ds(idx_ref[0, TOP_K * r + k], 1)],
                                     ybuf.at[buf, k, pl.ds(r, 1)], sem.at[buf])

    def start_tile(idx_ref, buf):
        def body(r, c):
            for k in range(TOP_K):
                row_copy(idx_ref, buf, r, k).start()
            return c
        lax.fori_loop(0, te, body, 0, unroll=ROW_UNROLL)

    cur = i % 2

    @pl.when(i == 0)
    def _():
        start_tile(slot_ref, 0)

    @pl.when(i + 1 < n)
    def _():
        start_tile(nslot_ref, 1 - cur)

    def wait_body(r, c):
        for k in range(TOP_K):
            row_copy(slot_ref, cur, r, k).wait()
        return c

    lax.fori_loop(0, te, wait_body, 0, unroll=ROW_UNROLL)
    w1 = rt_ref[:, TOP_K:TOP_K + 1]
    w2 = rt_ref[:, TOP_K + 1:TOP_K + 2]
    x2 = x_ref[...] + g2_ref[...] * (w1 * ybuf[cur, 0] + w2 * ybuf[cur, 1])
    x2_ref[...] = x2
    h = _rms(x2) * g_ref[...]
    if modulate:
        h = h * (1.0 + sc_ref[...]) + sh_ref[...]
    h_ref[...] = h.astype(h_ref.dtype)


def moe_combine(x1, ys, slot, route, g, mods_prev, mods_next, seg_ids, te, modulate, h_dtype):
    nt, d = x1.shape
    n = nt // te
    slot3 = slot.reshape(n, 1, TOP_K * te)
    gs = pltpu.PrefetchScalarGridSpec(
        num_scalar_prefetch=1, grid=(n,),
        in_specs=[pl.BlockSpec((None, 1, TOP_K * te), lambda i, s: (i, 0, 0), memory_space=pltpu.SMEM),
                  pl.BlockSpec((None, 1, TOP_K * te), lambda i, s: (jnp.minimum(i + 1, n - 1), 0, 0),
                               memory_space=pltpu.SMEM),
                  pl.BlockSpec((te, d), lambda i, s: (i, 0)),
                  pl.BlockSpec((te, LANES), lambda i, s: (i, 0)),
                  pl.BlockSpec((None, 1, d), lambda i, s: (s[i], 0, 5)),
                  pl.BlockSpec((1, d), lambda i, s: (0, 0)),
                  pl.BlockSpec((None, 1, d), lambda i, s: (s[i], 0, 1)),
                  pl.BlockSpec((None, 1, d), lambda i, s: (s[i], 0, 0)),
                  pl.BlockSpec(memory_space=pl.ANY)],
        out_specs=[pl.BlockSpec((te, d), lambda i, s: (i, 0)),
                   pl.BlockSpec((te, d), lambda i, s: (i, 0))],
        scratch_shapes=[pltpu.VMEM((2, TOP_K, te, d), F32), pltpu.SemaphoreType.DMA((2,))])
    return pl.pallas_call(
        functools.partial(_moe_combine_kernel, modulate=modulate), grid_spec=gs,
        out_shape=[jax.ShapeDtypeStruct((nt, d), F32), jax.ShapeDtypeStruct((nt, d), h_dtype)],
        compiler_params=_cparams(("arbitrary",)),
        name="moe_combine",
    )(seg_ids, slot3, slot3, x1, route, mods_prev, g.reshape(1, d), mods_next, mods_next, ys)


def kernel(x_prompt, x_sample, c, cache_na_k, cache_na_v, cache_gqa_k, cache_gqa_v, c_ctx, norm1_g, norm2_g, final_norm_g, ada_w, ada_b, w_in, w_out, na_rpb, q_norm_g, k_norm_g, hy_conv_w, hy_conv_b, hy_w1, hy_b1, hy_w2, hy_b2, hy_w3, hy_b3, hy_freq, hy_w_out, hy_bias, router_group_w, router_group_b, router_expert_w, router_expert_b, moe_w_gate, moe_w_up, moe_w_down):
    bc, lc, d = x_prompt.shape
    bl, ll, _ = x_sample.shape
    depth = w_in.shape[0]
    past = cache_na_k.shape[2]
    dh = HEAD_DIM
    na_w, gq_w = d // 4, d // 2
    hy_w = d - na_w - gq_w
    na_heads = na_w // dh
    nc = bc * lc
    nt = nc + bl * ll
    tm = 512
    moe_tm = 256
    assert nc % tm == 0 and ll % tm == 0 and past == lc

    te = 256

    def seg_of_tiles(t):
        return jnp.asarray(np.concatenate(
            [np.zeros(nc // t), 1 + np.repeat(np.arange(bl), ll // t)]).astype(np.int32))

    seg_e = seg_of_tiles(te)
    seg_m = seg_of_tiles(tm)
    nseg = 1 + bl
    cond = jnp.zeros((8, d), F32).at[0].set(c_ctx).at[1:1 + bl].set(c)
    mods_all = modulation_all(cond, ada_w, ada_b)

    ropes = rope_tables(ll)
    tabs_c = dft_tables(lc)
    tabs_l = dft_tables(ll)
    ck_na = cache_na_k.reshape(bl, depth, past, na_w)
    cv_na = cache_na_v.reshape(bl, depth, past, na_w)
    ck_gq = cache_gqa_k.reshape(bl, depth, past, GQA_KV_HEADS * dh)
    cv_gq = cache_gqa_v.reshape(bl, depth, past, GQA_KV_HEADS * dh)

    x = jnp.concatenate([x_prompt.reshape(nc, d), x_sample.reshape(bl * ll, d)], axis=0)
    mods = mods_all[0, :nseg].reshape(nseg, 1, 6 * d)
    h = norm_mod(x, norm1_g[0], mods, 1, 0, seg_e, te, BF16)
    st_na_k, st_na_v, st_gq_k, st_gq_v = [], [], [], []
    hy_col0 = 3 * na_w + gq_w + 2 * GQA_KV_HEADS * dh
    moe_x = jnp.zeros((nt * TOP_K + N_EXPERTS * moe_tm, d), F32)
    for l in range(depth):
        z = in_proj(h, w_in, l)
        zc = z[:nc]
        st_na_k.append(zc[:, na_w:2 * na_w].reshape(bc, lc, na_heads, dh))
        st_na_v.append(zc[:, 2 * na_w:3 * na_w].reshape(bc, lc, na_heads, dh))
        st_gq_v.append(zc[:, 3 * na_w + gq_w + GQA_KV_HEADS * dh:hy_col0].reshape(bc, lc, GQA_KV_HEADS, dh))

        ctx_a, ctx_b, bk = ctx_attn(z, q_norm_g[l], k_norm_g[l], bc, lc, na_w, gq_w)
        st_gq_k.append(bk.reshape(bc, lc, GQA_KV_HEADS, dh))

        bias = na_bias(na_rpb[l])
        lat_a = lat_na(z, ck_na, cv_na, l, bias, nc, bl, ll, na_w)
        q, k, v = lat_prep(z, ropes, q_norm_g[l], k_norm_g[l], nc, bl, ll, gq_w)
        lat_b = lat_gqa(q, k, v, ck_gq, cv_gq, l)

        mix_c = []
        for (row0, nb, L, tabs) in ((0, bc, lc, tabs_c), (nc, bl, ll, tabs_l)):
            ctab, s1tab, s2tab = tabs
            x0, zz, zb = hy_pre(z, hy_conv_w[l], hy_conv_b[l], row0, nb, L, hy_col0, hy_w)
            filt = hy_filter(L, hy_w1[l], hy_b1[l], hy_w2[l], hy_b2[l], hy_w3[l], hy_b3[l],
                             hy_freq[l], hy_w_out[l])
            kp, kq = dft_fwd(ctab, s1tab, filt)
            yr, yi = dft_mul(ctab, s1tab, zb, kp, kq)
            mix_c.append(dft_inv(ctab, s2tab, yr, yi, x0, zz, hy_bias[l]))

        x1, h2, route = out_proj_router((ctx_a, ctx_b, mix_c[0]), (lat_a, lat_b, mix_c[1]),
                                        w_out[l].astype(BF16), x, norm2_g[l], mods, seg_e, te,
                                        router_group_w[l], router_group_b[l],
                                        router_expert_w[l], router_expert_b[l])
        meta, slot = moe_plan(route, moe_tm)
        moe_x = moe_dispatch(h2, slot, moe_x, tm)
        ys = moe_experts(moe_x, meta, moe_w_gate, moe_w_up, moe_w_down, l, moe_tm)
        last = l == depth - 1
        mods_next = mods if last else mods_all[l + 1, :nseg].reshape(nseg, 1, 6 * d)
        g_next = final_norm_g if last else norm1_g[l + 1]
        x, h = moe_combine(x1, ys, slot, route, g_next, mods, mods_next, seg_m, tm, not last,
                           F32 if last else BF16)
        mods = mods_next

    y_prompt = h[:nc].reshape(bc, lc, d)
    y_sample = h[nc:].reshape(bl, ll, d)
    return (y_prompt, y_sample, jnp.stack(st_na_k, axis=1), jnp.stack(st_na_v, axis=1),
            jnp.stack(st_gq_k, axis=1), jnp.stack(st_gq_v, axis=1))
```

```python
import functools
import math

import numpy as np
import jax
import jax.numpy as jnp
from jax import lax
from jax.experimental import pallas as pl
from jax.experimental.pallas import tpu as pltpu

F32 = jnp.float32
BF16 = jnp.bfloat16
HIGHEST = lax.Precision.HIGHEST

GRID_W = 64
HEAD_DIM = 128
NA_WIN_R = 8
NA_WIN_C = 16
GQA_KV_HEADS = 2
ROPE_THETA = 10000.0
HY_EMB = 33
HY_HID = 64
HY_FAST_DECAY = 0.3
HY_SLOW_DECAY = 1.5
HY_TARGET = 0.01
N_GROUPS = 4
EXP_PER_GROUP = 4
N_EXPERTS = N_GROUPS * EXP_PER_GROUP
TOP_K = 2
EPS = 1e-6
NEG_INF = -1e30

LANES = 128
VMEM_LIMIT = 52 * 1024 * 1024


def _cparams(sem, vmem=VMEM_LIMIT):
    return pltpu.CompilerParams(dimension_semantics=sem, vmem_limit_bytes=vmem)


def _dot(a, b):
    return jnp.dot(a, b, preferred_element_type=F32)


def _dot_t(a, b):
    return lax.dot_general(a, b, (((1,), (1,)), ((), ())), preferred_element_type=F32)


def _dot_hi(a, b):
    return jnp.dot(a, b, preferred_element_type=F32, precision=HIGHEST)


def _mod_kernel(c_ref, w_ref, b_ref, o_ref):
    c = c_ref[...]
    s = c * (1.0 / (1.0 + jnp.exp(-c)))
    o_ref[...] = _dot_hi(s, w_ref[...]) + b_ref[...]


def modulation_all(cond, ada_w, ada_b, tn=1536):
    depth, d, n = ada_w.shape
    r = cond.shape[0]
    return pl.pallas_call(
        _mod_kernel,
        grid=(depth, n // tn),
        in_specs=[pl.BlockSpec((r, d), lambda l, j: (0, 0)),
                  pl.BlockSpec((None, d, tn), lambda l, j: (l, 0, j)),
                  pl.BlockSpec((None, 1, tn), lambda l, j: (l, 0, j))],
        out_specs=pl.BlockSpec((None, r, tn), lambda l, j: (l, 0, j)),
        out_shape=jax.ShapeDtypeStruct((depth, r, n), F32),
        compiler_params=_cparams(("arbitrary", "arbitrary")),
        name="modulation",
    )(cond, ada_w, ada_b.reshape(depth, 1, n))


def _rms(x):
    return x * lax.rsqrt(jnp.mean(x * x, axis=-1, keepdims=True) + EPS)


def _norm_mod_kernel(s_ref, x_ref, g_ref, sc_ref, sh_ref, o_ref):
    h = _rms(x_ref[...]) * g_ref[...]
    o_ref[...] = (h * (1.0 + sc_ref[...]) + sh_ref[...]).astype(o_ref.dtype)


def norm_mod(x, g, mods, sc_idx, sh_idx, seg_ids, tm, out_dtype):
    nt, d = x.shape
    gs = pltpu.PrefetchScalarGridSpec(
        num_scalar_prefetch=1, grid=(nt // tm,),
        in_specs=[pl.BlockSpec((tm, d), lambda i, s: (i, 0)),
                  pl.BlockSpec((1, d), lambda i, s: (0, 0)),
                  pl.BlockSpec((None, 1, d), lambda i, s: (s[i], 0, sc_idx)),
                  pl.BlockSpec((None, 1, d), lambda i, s: (s[i], 0, sh_idx))],
        out_specs=pl.BlockSpec((tm, d), lambda i, s: (i, 0)))
    return pl.pallas_call(
        _norm_mod_kernel, grid_spec=gs,
        out_shape=jax.ShapeDtypeStruct((nt, d), out_dtype),
        compiler_params=_cparams(("arbitrary",)),
        name="norm_mod",
    )(seg_ids, x, g.reshape(1, d), mods, mods)


def _mm_kernel(a_ref, w_ref, o_ref, wbf_ref):
    @pl.when(pl.program_id(1) == 0)
    def _():
        wbf_ref[...] = w_ref[...].astype(BF16)

    o_ref[...] = _dot(a_ref[...], wbf_ref[...])


def in_proj(h, w_in, layer, tm=1024, tn=768):
    nt, d = h.shape
    tm = math.gcd(tm, nt)
    n = w_in.shape[2]
    return pl.pallas_call(
        _mm_kernel,
        grid=(n // tn, nt // tm),
        in_specs=[pl.BlockSpec((tm, d), lambda j, i: (i, 0)),
                  pl.BlockSpec((None, d, tn), lambda j, i: (layer, 0, j))],
        out_specs=pl.BlockSpec((tm, tn), lambda j, i: (i, j)),
        out_shape=jax.ShapeDtypeStruct((nt, n), F32),
        scratch_shapes=[pltpu.VMEM((d, tn), BF16)],
        compiler_params=_cparams(("arbitrary", "arbitrary")),
        name="in_proj",
    )(h, w_in)


def _softmax_pv(s, v):
    m = jnp.max(s, axis=-1, keepdims=True)
    p = jnp.exp(s - m)
    l = jnp.sum(p, axis=-1, keepdims=True)
    return _dot(p.astype(BF16), v) / l


def _ctx_attn_kernel(za_ref, zb_ref, qg_ref, kg_ref, a_ref, b_ref, bk_ref, *, na_heads, gq_heads):
    dh = HEAD_DIM
    scale = dh ** -0.5
    L = za_ref.shape[0]
    na_w = na_heads * dh
    for h in range(na_heads):
        q = za_ref[:, h * dh:(h + 1) * dh].astype(BF16)
        k = za_ref[:, na_w + h * dh:na_w + (h + 1) * dh].astype(BF16)
        v = za_ref[:, 2 * na_w + h * dh:2 * na_w + (h + 1) * dh].astype(BF16)
        s = _dot_t(q, k) * scale
        a_ref[:, h * dh:(h + 1) * dh] = _softmax_pv(s, v).astype(a_ref.dtype)
    gq_w = gq_heads * dh
    group = gq_heads // GQA_KV_HEADS
    for kv in range(GQA_KV_HEADS):
        kn = _rms(zb_ref[:, gq_w + kv * dh:gq_w + (kv + 1) * dh]) * kg_ref[...]
        bk_ref[:, kv * dh:(kv + 1) * dh] = kn
        knb = kn.astype(BF16)
        v = zb_ref[:, gq_w + (GQA_KV_HEADS + kv) * dh:gq_w + (GQA_KV_HEADS + kv + 1) * dh].astype(BF16)
        for g in range(group):
            h = kv * group + g
            qn = (_rms(zb_ref[:, h * dh:(h + 1) * dh]) * qg_ref[...]).astype(BF16)
            s = _dot_t(qn, knb) * scale
            b_ref[:, h * dh:(h + 1) * dh] = _softmax_pv(s, v).astype(b_ref.dtype)


def ctx_attn(z, q_g, k_g, bc, lc, na_w, gq_w):
    dh = HEAD_DIM
    wa = 3 * na_w
    wb = gq_w + 2 * GQA_KV_HEADS * dh
    assert wa == wb
    kern = functools.partial(_ctx_attn_kernel, na_heads=na_w // dh, gq_heads=gq_w // dh)
    return pl.pallas_call(
        kern, grid=(bc,),
        in_specs=[pl.BlockSpec((lc, wa), lambda b: (b, 0)),
                  pl.BlockSpec((lc, wb), lambda b: (b, 1)),
                  pl.BlockSpec((1, dh), lambda b: (0, 0)),
                  pl.BlockSpec((1, dh), lambda b: (0, 0))],
        out_specs=[pl.BlockSpec((lc, na_w), lambda b: (b, 0)),
                   pl.BlockSpec((lc, gq_w), lambda b: (b, 0)),
                   pl.BlockSpec((lc, GQA_KV_HEADS * dh), lambda b: (b, 0))],
        out_shape=[jax.ShapeDtypeStruct((bc * lc, na_w), BF16),
                   jax.ShapeDtypeStruct((bc * lc, gq_w), BF16),
                   jax.ShapeDtypeStruct((bc * lc, GQA_KV_HEADS * dh), F32)],
        compiler_params=_cparams(("arbitrary",)),
        name="ctx_attn",
    )(z, z, q_g.reshape(1, dh), k_g.reshape(1, dh))


def _rope_kernel(cos_ref, sa_ref, sb_ref):
    n, dh = cos_ref.shape
    quarter = dh // 2
    t = lax.broadcasted_iota(jnp.int32, (n, dh), 0)
    lane = lax.broadcasted_iota(jnp.int32, (n, dh), 1)
    f = (lane & (quarter // 2 - 1)).astype(F32)
    inv = jnp.exp(-(2.0 * f / quarter) * math.log(ROPE_THETA))
    pos = jnp.where(lane < quarter, t >> int(math.log2(GRID_W)), t & (GRID_W - 1)).astype(F32)
    ang = pos * inv
    sin = jnp.sin(ang)
    first = (lane & (quarter - 1)) < (quarter // 2)
    cos_ref[...] = jnp.cos(ang)
    sa_ref[...] = jnp.where(first, -sin, 0.0)
    sb_ref[...] = jnp.where(first, 0.0, sin)


def rope_tables(n):
    shp = jax.ShapeDtypeStruct((n, HEAD_DIM), F32)
    return pl.pallas_call(_rope_kernel, out_shape=[shp, shp, shp], name="rope_tables")()


def _rope(x, cos, sa, sb):
    q4 = HEAD_DIM // 4
    return x * cos + pltpu.roll(x, HEAD_DIM - q4, 1) * sa + pltpu.roll(x, q4, 1) * sb


def _lat_prep_kernel(z_ref, cos_ref, sa_ref, sb_ref, qg_ref, kg_ref, q_ref, k_ref, v_ref, *, gq_heads):
    dh = HEAD_DIM
    scale = dh ** -0.5
    cos, sa, sb = cos_ref[...], sa_ref[...], sb_ref[...]
    for h in range(gq_heads):
        qn = _rms(z_ref[:, h * dh:(h + 1) * dh]) * qg_ref[...]
        q_ref[h] = (_rope(qn, cos, sa, sb) * scale).astype(BF16)
    gq_w = gq_heads * dh
    for kv in range(GQA_KV_HEADS):
        kn = _rms(z_ref[:, gq_w + kv * dh:gq_w + (kv + 1) * dh]) * kg_ref[...]
        k_ref[kv] = _rope(kn, cos, sa, sb).astype(BF16)
        v_ref[kv] = z_ref[:, gq_w + (GQA_KV_HEADS + kv) * dh:gq_w + (GQA_KV_HEADS + kv + 1) * dh].astype(BF16)


def lat_prep(z, ropes, q_g, k_g, nc, bl, ll, gq_w, tl=512):
    dh = HEAD_DIM
    gq_heads = gq_w // dh
    wb = gq_w + 2 * GQA_KV_HEADS * dh
    nb = ll // tl
    off = nc // tl
    cos, sa, sb = ropes
    tab = pl.BlockSpec((tl, dh), lambda b, i: (i, 0))
    one = pl.BlockSpec((1, dh), lambda b, i: (0, 0))
    return pl.pallas_call(
        functools.partial(_lat_prep_kernel, gq_heads=gq_heads),
        grid=(bl, nb),
        in_specs=[pl.BlockSpec((tl, wb), lambda b, i: (off + b * nb + i, 1)), tab, tab, tab, one, one],
        out_specs=[pl.BlockSpec((None, gq_heads, tl, dh), lambda b, i: (b, 0, i, 0)),
                   pl.BlockSpec((None, GQA_KV_HEADS, tl, dh), lambda b, i: (b, 0, i, 0)),
                   pl.BlockSpec((None, GQA_KV_HEADS, tl, dh), lambda b, i: (b, 0, i, 0))],
        out_shape=[jax.ShapeDtypeStruct((bl, gq_heads, ll, dh), BF16),
                   jax.ShapeDtypeStruct((bl, GQA_KV_HEADS, ll, dh), BF16),
                   jax.ShapeDtypeStruct((bl, GQA_KV_HEADS, ll, dh), BF16)],
        compiler_params=_cparams(("arbitrary", "arbitrary")),
        name="lat_prep",
    )(z, cos, sa, sb, q_g.reshape(1, dh), k_g.reshape(1, dh))


def _lat_gqa_kernel(q_ref, k_ref, v_ref, ck_ref, cv_ref, o_ref, *, tk):
    group, tq, dh = q_ref.shape
    rows = group * tq
    q = q_ref[...].reshape(rows, dh)
    ll = k_ref.shape[0]

    def update(carry, kc, vc):
        m, l, acc = carry
        s = _dot_t(q, kc)
        m_new = jnp.maximum(m, jnp.max(s, axis=-1, keepdims=True))
        alpha = jnp.exp(m - m_new)
        p = jnp.exp(s - m_new)
        l = alpha * l + jnp.sum(p, axis=-1, keepdims=True)
        acc = alpha * acc + _dot(p.astype(BF16), vc)
        return m_new, l, acc

    def body(c, carry):
        st = pl.multiple_of(c * tk, tk)
        return update(carry, k_ref[pl.ds(st, tk), :], v_ref[pl.ds(st, tk), :])

    init = (jnp.full((rows, 1), -jnp.inf, F32), jnp.zeros((rows, 1), F32), jnp.zeros((rows, dh), F32))
    carry = lax.fori_loop(0, ll // tk, body, init)
    m, l, acc = update(carry, ck_ref[...].astype(BF16), cv_ref[...].astype(BF16))
    out = acc / l
    for g in range(group):
        o_ref[:, g * dh:(g + 1) * dh] = out[g * tq:(g + 1) * tq].astype(o_ref.dtype)


def lat_gqa(q, k, v, cache_k, cache_v, layer, tq=256, tk=4096):
    bl, gq_heads, ll, dh = q.shape
    group = gq_heads // GQA_KV_HEADS
    past = cache_k.shape[2]
    nq = ll // tq
    tk = min(tk, ll)
    assert ll % tk == 0
    ck = pl.BlockSpec((None, None, past, dh), lambda b, h, i: (b, layer, 0, h))
    return pl.pallas_call(
        functools.partial(_lat_gqa_kernel, tk=tk),
        grid=(bl, GQA_KV_HEADS, nq),
        in_specs=[pl.BlockSpec((None, group, tq, dh), lambda b, h, i: (b, h, i, 0)),
                  pl.BlockSpec((None, None, ll, dh), lambda b, h, i: (b, h, 0, 0)),
                  pl.BlockSpec((None, None, ll, dh), lambda b, h, i: (b, h, 0, 0)),
                  ck, ck],
        out_specs=pl.BlockSpec((tq, group * dh), lambda b, h, i: (b * nq + i, h)),
        out_shape=jax.ShapeDtypeStruct((bl * ll, gq_heads * dh), BF16),
        compiler_params=_cparams(("arbitrary", "arbitrary", "arbitrary")),
        name="lat_gqa",
    )(q, k, v, cache_k, cache_v)


def _na_bias_kernel(rpb_ref, o_ref):
    h = pl.program_id(0)
    w = GRID_W
    qc = lax.broadcasted_iota(jnp.int32, (w, w), 0)
    kc = lax.broadcasted_iota(jnp.int32, (w, w), 1)
    ci = jnp.clip(kc - qc + NA_WIN_C - 1, 0, 2 * NA_WIN_C - 2)
    cs = jnp.clip(qc - NA_WIN_C // 2, 0, w - NA_WIN_C)
    col_ok = (kc >= cs) & (kc < cs + NA_WIN_C)
    nr, ncol = 2 * NA_WIN_R - 1, 2 * NA_WIN_C - 1
    tiles = []
    for dr in range(nr):
        t = jnp.zeros((w, w), F32)
        for c in range(ncol):
            t = jnp.where(ci == c, rpb_ref[h, dr * ncol + c], t)
        tiles.append(jnp.where(col_ok, t, NEG_INF))
    for d0 in range(NA_WIN_R):
        for j in range(NA_WIN_R):
            o_ref[d0, :, j * w:(j + 1) * w] = tiles[d0 + j]


def na_bias(rpb_l):
    heads = rpb_l.shape[0]
    flat = rpb_l.reshape(heads, -1)
    return pl.pallas_call(
        _na_bias_kernel, grid=(heads,),
        in_specs=[pl.BlockSpec(memory_space=pltpu.SMEM)],
        out_specs=pl.BlockSpec((None, NA_WIN_R, GRID_W, NA_WIN_R * GRID_W), lambda h: (h, 0, 0, 0)),
        out_shape=jax.ShapeDtypeStruct((heads, NA_WIN_R, GRID_W, NA_WIN_R * GRID_W), F32),
        compiler_params=_cparams(("arbitrary",)),
        name="na_bias",
    )(flat)


def _lat_na_kernel(q_ref, k_ref, v_ref, ck_ref, cv_ref, bias_ref, o_ref, kb_ref, vb_ref, *, rb, rows):
    w = GRID_W
    scale = HEAD_DIM ** -0.5
    i = pl.program_id(2)

    @pl.when(i == 0)
    def _():
        kb_ref[...] = k_ref[...].astype(BF16)
        vb_ref[...] = v_ref[...].astype(BF16)

    ck = ck_ref[...].astype(BF16)
    cv = cv_ref[...].astype(BF16)
    qs = (q_ref[...] * scale).astype(BF16)
    s2 = _dot_t(qs, ck)
    s1_rows, v_wins = [], []
    for j in range(rb):
        r = i * rb + j
        rs = jnp.clip(r - NA_WIN_R // 2, 0, rows - NA_WIN_R)
        d0 = rs - r + NA_WIN_R - 1
        st = pl.multiple_of(rs * w, w)
        kw = kb_ref[pl.ds(st, NA_WIN_R * w), :]
        v_wins.append(vb_ref[pl.ds(st, NA_WIN_R * w), :])
        s1_rows.append(_dot_t(qs[j * w:(j + 1) * w], kw) + bias_ref[d0])
    s1 = jnp.concatenate(s1_rows, axis=0)
    m = jnp.maximum(jnp.max(s1, axis=-1, keepdims=True), jnp.max(s2, axis=-1, keepdims=True))
    p1 = jnp.exp(s1 - m)
    p2 = jnp.exp(s2 - m)
    l = jnp.sum(p1, axis=-1, keepdims=True) + jnp.sum(p2, axis=-1, keepdims=True)
    p1 = p1.astype(BF16)
    o2 = _dot(p2.astype(BF16), cv)
    o1 = jnp.concatenate([_dot(p1[j * w:(j + 1) * w], v_wins[j]) for j in range(rb)], axis=0)
    o_ref[...] = ((o1 + o2) / l).astype(o_ref.dtype)


def lat_na(z, cache_k, cache_v, layer, bias, nc, bl, ll, na_w, rb=32):
    dh = HEAD_DIM
    heads = na_w // dh
    rows = ll // GRID_W
    rb = min(rb, rows)
    assert rows >= NA_WIN_R and nc % ll == 0 and rows % rb == 0
    past = cache_k.shape[2]
    nb = rows // rb
    tq = rb * GRID_W
    off = nc // tq
    lat0 = nc // ll
    ck = pl.BlockSpec((None, None, past, dh), lambda b, h, i: (b, layer, 0, h))
    return pl.pallas_call(
        functools.partial(_lat_na_kernel, rb=rb, rows=rows),
        grid=(bl, heads, nb),
        in_specs=[pl.BlockSpec((tq, dh), lambda b, h, i: (off + b * nb + i, h)),
                  pl.BlockSpec((ll, dh), lambda b, h, i: (lat0 + b, heads + h)),
                  pl.BlockSpec((ll, dh), lambda b, h, i: (lat0 + b, 2 * heads + h)),
                  ck, ck,
                  pl.BlockSpec((None, NA_WIN_R, GRID_W, NA_WIN_R * GRID_W), lambda b, h, i: (h, 0, 0, 0))],
        out_specs=pl.BlockSpec((tq, dh), lambda b, h, i: (b * nb + i, h)),
        out_shape=jax.ShapeDtypeStruct((bl * ll, na_w), BF16),
        scratch_shapes=[pltpu.VMEM((ll, dh), BF16), pltpu.VMEM((ll, dh), BF16)],
        compiler_params=_cparams(("arbitrary", "arbitrary", "arbitrary")),
        name="lat_na",
    )(z, z, z, cache_k, cache_v, bias)


def _hy_pre_kernel(u0_ref, u1_ref, u2_ref, w0_ref, w1_ref, w2_ref, b0_ref, b1_ref, b2_ref,
                   x0_ref, zz_ref, zb_ref):
    L = u0_ref.shape[0]
    t = lax.broadcasted_iota(jnp.int32, u0_ref.shape, 0)

    def conv(u_ref, w_ref, b_ref):
        u = u_ref[...]
        prev = jnp.where(t == 0, 0.0, pltpu.roll(u, 1, 0))
        nxt = jnp.where(t == L - 1, 0.0, pltpu.roll(u, L - 1, 0))
        return prev * w_ref[0:1, :] + u * w_ref[1:2, :] + nxt * w_ref[2:3, :] + b_ref[...]

    x0_ref[...] = conv(u0_ref, w0_ref, b0_ref)
    zz = conv(u2_ref, w2_ref, b2_ref) * conv(u1_ref, w1_ref, b1_ref)
    zz_ref[...] = zz
    zb_ref[...] = zz.astype(BF16)


def hy_pre(z, conv_w, conv_b, row0, nb, L, col0, hy_w):
    cb = LANES
    nj = hy_w // cb
    c0 = col0 // cb
    r0 = row0 // L
    assert row0 % L == 0
    u = lambda part: pl.BlockSpec((L, cb), lambda b, j: (r0 + b, c0 + part * nj + j))
    wspec = lambda part: pl.BlockSpec((3, cb), lambda b, j: (0, part * nj + j))
    bspec = lambda part: pl.BlockSpec((1, cb), lambda b, j: (0, part * nj + j))
    o = pl.BlockSpec((None, L, cb), lambda b, j: (b, 0, j))
    return pl.pallas_call(
        _hy_pre_kernel, grid=(nb, nj),
        in_specs=[u(0), u(1), u(2), wspec(0), wspec(1), wspec(2), bspec(0), bspec(1), bspec(2)],
        out_specs=[o, o, o],
        out_shape=[jax.ShapeDtypeStruct((nb, L, hy_w), F32), jax.ShapeDtypeStruct((nb, L, hy_w), F32),
                   jax.ShapeDtypeStruct((nb, L, hy_w), BF16)],
        compiler_params=_cparams(("arbitrary", "arbitrary")),
        name="hy_pre",
    )(z, z, z, conv_w, conv_w, conv_w, conv_b.reshape(1, -1), conv_b.reshape(1, -1), conv_b.reshape(1, -1))


def _hy_filter_kernel(w1_ref, b1_ref, w2_ref, b2_ref, w3_ref, b3_ref, fq_ref, wo_ref, o_ref, *, hy_w):
    L = o_ref.shape[1]
    bands = (HY_EMB - 1) // 2
    ti = lax.broadcasted_iota(jnp.int32, (L, LANES), 0).astype(F32)
    lane = lax.broadcasted_iota(jnp.int32, (L, LANES), 1)
    t01 = ti / (L - 1)
    w = (2.0 * math.pi / L) * ti
    band = ((lane - 1) % bands).astype(F32)
    fr = 1e-4 + band * ((bands - 1 - 1e-4) / (bands - 1))
    ang = fr * w
    feat = jnp.where(lane == 0, t01,
                     jnp.where(lane <= bands, jnp.cos(ang),
                               jnp.where(lane <= 2 * bands, -jnp.sin(ang), 0.0)))
    fq = fq_ref[...]
    hdn = jnp.sin(fq * (_dot_hi(feat, w1_ref[...]) + b1_ref[...]))
    hdn = jnp.sin(fq * (_dot_hi(hdn, w2_ref[...]) + b2_ref[...]))
    hdn = jnp.sin(fq * (_dot_hi(hdn, w3_ref[...]) + b3_ref[...]))
    filt = _dot_hi(hdn, wo_ref[...])
    max_decay = math.log(HY_TARGET) / HY_FAST_DECAY
    min_decay = math.log(HY_TARGET) / HY_SLOW_DECAY
    ch = lax.broadcasted_iota(jnp.int32, (L, hy_w), 1).astype(F32)
    deltas = jnp.abs(min_decay + ch * ((max_decay - min_decay) / (hy_w - 1)))
    tt = lax.broadcasted_iota(jnp.int32, (L, hy_w), 0)
    decay = jnp.exp(-(tt.astype(F32) / (L - 1)) * deltas)
    o_ref[0] = (filt[:, :hy_w] * decay).astype(o_ref.dtype)
    o_ref[1] = jnp.where(tt == 0, 0.0, filt[:, hy_w:] * decay).astype(o_ref.dtype)


def _pad2(a, r, c):
    return jnp.pad(a, ((0, r - a.shape[0]), (0, c - a.shape[1])))


def hy_filter(L, w1, b1, w2, b2, w3, b3, freq, w_out):
    hy_w = w_out.shape[1] // 2
    p = LANES
    args = (_pad2(w1, p, p), _pad2(b1[None], 1, p), _pad2(w2, p, p), _pad2(b2[None], 1, p),
            _pad2(w3, p, p), _pad2(b3[None], 1, p), _pad2(freq[None], 1, p), _pad2(w_out, p, 2 * hy_w))
    return pl.pallas_call(
        functools.partial(_hy_filter_kernel, hy_w=hy_w),
        out_shape=jax.ShapeDtypeStruct((2, L, hy_w), BF16),
        compiler_params=_cparams(None),
        name="hy_filter",
    )(*args)


def _dft_tables_kernel(c_ref, s1_ref, s2_ref, *, L):
    tk = c_ref.shape[0]
    n2 = 2 * L
    theta = 2.0 * math.pi / n2
    k = pl.program_id(0) * tk + lax.broadcasted_iota(jnp.int32, (tk, LANES), 0)
    lane = lax.broadcasted_iota(jnp.int32, (tk, LANES), 1)
    a = ((k * lane) & (n2 - 1)).astype(F32) * theta
    b = ((k * LANES * lane) & (n2 - 1)).astype(F32) * theta
    ca, sa, cb, sb = jnp.cos(a), jnp.sin(a), jnp.cos(b), jnp.sin(b)
    sign_k = (1 - 2 * (k & 1)).astype(F32)
    for j in range(L // LANES):
        cbj = cb[:, j:j + 1]
        sbj = sb[:, j:j + 1]
        c = ca * cbj - sa * sbj
        s = sa * cbj + ca * sbj
        t = lane + j * LANES
        sign_t = (1 - 2 * (t & 1)).astype(F32)
        c_ref[:, j * LANES:(j + 1) * LANES] = c.astype(BF16)
        s1_ref[:, j * LANES:(j + 1) * LANES] = jnp.where(k == 0, sign_t, s).astype(BF16)
        s2_ref[:, j * LANES:(j + 1) * LANES] = jnp.where(t == 0, sign_k, s).astype(BF16)


def dft_tables(L, tk=256):
    tk = min(tk, L)
    shp = jax.ShapeDtypeStruct((L, L), BF16)
    spec = pl.BlockSpec((tk, L), lambda i: (i, 0))
    return pl.pallas_call(
        functools.partial(_dft_tables_kernel, L=L), grid=(L // tk,),
        out_specs=[spec, spec, spec], out_shape=[shp, shp, shp],
        compiler_params=_cparams(("arbitrary",)),
        name="dft_tables",
    )()


def _dft_fwd_kernel(c_ref, s_ref, z_ref, re_ref, im_ref):
    z = z_ref[...]
    re_ref[...] = _dot(c_ref[...], z)
    im_ref[...] = _dot(s_ref[...], z)


def dft_fwd(ctab, stab, zb, tm=512):
    nb, L, cw = zb.shape
    tm = min(tm, L)
    tab = pl.BlockSpec((tm, L), lambda i, b: (i, 0))
    o = pl.BlockSpec((None, tm, cw), lambda i, b: (b, i, 0))
    shp = jax.ShapeDtypeStruct((nb, L, cw), F32)
    return pl.pallas_call(
        _dft_fwd_kernel, grid=(L // tm, nb),
        in_specs=[tab, tab, pl.BlockSpec((None, L, cw), lambda i, b: (b, 0, 0))],
        out_specs=[o, o], out_shape=[shp, shp],
        compiler_params=_cparams(("arbitrary", "arbitrary")),
        name="dft_fwd",
    )(ctab, stab, zb)


def _dft_mul_kernel(c_ref, s_ref, z_ref, kp_ref, kq_ref, yr_ref, yi_ref, *, n2):
    tl = c_ref.shape[0]
    z = z_ref[...]
    xr = _dot(c_ref[...], z)
    xs = _dot(s_ref[...], z)
    k = pl.program_id(0) * tl + lax.broadcasted_iota(jnp.int32, xr.shape, 0)
    kre = kp_ref[0] + kp_ref[1]
    kim = kq_ref[1] - kq_ref[0]
    knyq = kq_ref[0] + kq_ref[1]
    first = k == 0
    yre = xr * kre + jnp.where(first, 0.0, xs * kim)
    yim = xr * kim - xs * kre
    yr_ref[...] = (jnp.where(first, 1.0 / n2, 2.0 / n2) * yre).astype(BF16)
    yi_ref[...] = jnp.where(first, xs * knyq * (1.0 / n2), (-2.0 / n2) * yim).astype(BF16)


def dft_mul(ctab, stab, zb, kp, kq, tm=512):
    nb, L, cw = zb.shape
    tm = min(tm, L)
    tab = pl.BlockSpec((tm, L), lambda i, b: (i, 0))
    f = pl.BlockSpec((2, tm, cw), lambda i, b: (0, i, 0))
    o = pl.BlockSpec((None, tm, cw), lambda i, b: (b, i, 0))
    shp = jax.ShapeDtypeStruct((nb, L, cw), BF16)
    return pl.pallas_call(
        functools.partial(_dft_mul_kernel, n2=2 * L), grid=(L // tm, nb),
        in_specs=[tab, tab, pl.BlockSpec((None, L, cw), lambda i, b: (b, 0, 0)), f, f],
        out_specs=[o, o], out_shape=[shp, shp],
        compiler_params=_cparams(("arbitrary", "arbitrary")),
        name="dft_mul",
    )(ctab, stab, zb, kp, kq)


def _dft_inv_kernel(c_ref, s_ref, yr_ref, yi_ref, x0_ref, zz_ref, bd_ref, o_ref):
    y = _dot(c_ref[...], yr_ref[...]) + _dot(s_ref[...], yi_ref[...])
    o_ref[...] = (x0_ref[...] * (y + zz_ref[...] * bd_ref[...])).astype(o_ref.dtype)


def dft_inv(ctab, s2tab, yr, yi, x0, zz, bias_d, tm=512):
    nb, L, cw = yr.shape
    tm = min(tm, L)
    nti = L // tm
    tab = pl.BlockSpec((tm, L), lambda i, b: (i, 0))
    full = pl.BlockSpec((None, L, cw), lambda i, b: (b, 0, 0))
    tile = pl.BlockSpec((None, tm, cw), lambda i, b: (b, i, 0))
    return pl.pallas_call(
        _dft_inv_kernel, grid=(nti, nb),
        in_specs=[tab, tab, full, full, tile, tile, pl.BlockSpec((1, cw), lambda i, b: (0, 0))],
        out_specs=pl.BlockSpec((tm, cw), lambda i, b: (b * nti + i, 0)),
        out_shape=jax.ShapeDtypeStruct((nb * L, cw), BF16),
        compiler_params=_cparams(("arbitrary", "arbitrary")),
        name="dft_inv",
    )(ctab, s2tab, yr, yi, x0, zz, bias_d.reshape(1, cw))


def _split_bf16(a):
    hi = a.astype(BF16)
    return hi, (a - hi.astype(F32)).astype(BF16)


def _dot_3pass(a, b):
    a_hi, a_lo = _split_bf16(a)
    b_hi, b_lo = _split_bf16(b)
    return _dot(a_hi, b_hi) + (_dot(a_lo, b_hi) + _dot(a_hi, b_lo))


def _route(logits):
    li = lax.broadcasted_iota(jnp.int32, logits.shape, 1)
    big = jnp.int32(LANES)
    ninf = -jnp.inf

    def rmax(x):
        return jnp.max(x, axis=-1, keepdims=True)

    def first_at(x, m):
        return jnp.min(jnp.where(x == m, li, big), axis=-1, keepdims=True)

    gmask = li < N_GROUPS
    gl = jnp.where(gmask, logits, ninf)
    gmax = rmax(gl)
    g_sel = first_at(gl, gmax)
    g_prob = 1.0 / jnp.sum(jnp.where(gmask, jnp.exp(gl - gmax), 0.0), axis=-1, keepdims=True)
    e = li - N_GROUPS
    emask = (e >= 0) & (e < N_EXPERTS) & ((e >> 2) == g_sel)
    el = jnp.where(emask, logits, ninf)
    emax = rmax(el)
    esum = jnp.sum(jnp.where(emask, jnp.exp(el - emax), 0.0), axis=-1, keepdims=True)
    l1 = first_at(el, emax)
    el2 = jnp.where(li == l1, ninf, el)
    emax2 = rmax(el2)
    l2 = first_at(el2, emax2)
    p1 = 1.0 / esum
    p2 = jnp.exp(emax2 - emax) / esum
    tot = p1 + p2
    w1 = p1 / tot * g_prob
    w2 = p2 / tot * g_prob
    return jnp.where(li == 0, (l1 - N_GROUPS).astype(F32),
                     jnp.where(li == 1, (l2 - N_GROUPS).astype(F32),
                               jnp.where(li == 2, w1, jnp.where(li == 3, w2, 0.0))))


def _out_router_kernel(s_ref, ca_ref, cb_ref, cc_ref, la_ref, lb_ref, lc_ref, w_ref, x_ref, g1_ref,
                       g_ref, sc_ref, sh_ref, wr_ref, br_ref, x1_ref, h_ref, rt_ref, *, nci):
    i = pl.program_id(0)

    def project(a_ref, b_ref, c_ref):
        wa = a_ref.shape[1]
        wb = b_ref.shape[1]
        acc = _dot(a_ref[...], w_ref[0:wa, :])
        acc += _dot(b_ref[...], w_ref[wa:wa + wb, :])
        acc += _dot(c_ref[...], w_ref[wa + wb:, :])
        x1_ref[...] = x_ref[...] + g1_ref[...] * acc

    @pl.when(i < nci)
    def _():
        project(ca_ref, cb_ref, cc_ref)

    @pl.when(i >= nci)
    def _():
        project(la_ref, lb_ref, lc_ref)

    h = _rms(x1_ref[...]) * g_ref[...]
    h = h * (1.0 + sc_ref[...]) + sh_ref[...]
    h_ref[...] = h
    rt_ref[...] = _route(_dot_3pass(h, wr_ref[...]) + br_ref[...])


def out_proj_router(mix_ctx, mix_lat, w_out_bf, x, g, mods, seg_ids, tm, wg, bg, we, be):
    nt, d = x.shape
    nci = mix_ctx[0].shape[0] // tm
    wr = _pad2(jnp.concatenate([wg, we], axis=1), d, LANES)
    br = _pad2(jnp.concatenate([bg, be])[None], 1, LANES)
    cspec = lambda a: pl.BlockSpec((tm, a.shape[1]), lambda i, s: (jnp.minimum(i, nci - 1), 0))
    lspec = lambda a: pl.BlockSpec((tm, a.shape[1]), lambda i, s: (jnp.maximum(i - nci, 0), 0))
    row = pl.BlockSpec((tm, d), lambda i, s: (i, 0))
    mod = lambda p: pl.BlockSpec((None, 1, d), lambda i, s: (s[i], 0, p))
    gs = pltpu.PrefetchScalarGridSpec(
        num_scalar_prefetch=1, grid=(nt // tm,),
        in_specs=[cspec(mix_ctx[0]), cspec(mix_ctx[1]), cspec(mix_ctx[2]),
                  lspec(mix_lat[0]), lspec(mix_lat[1]), lspec(mix_lat[2]),
                  pl.BlockSpec((d, d), lambda i, s: (0, 0)),
                  row, mod(2),
                  pl.BlockSpec((1, d), lambda i, s: (0, 0)), mod(4), mod(3),
                  pl.BlockSpec((d, LANES), lambda i, s: (0, 0)),
                  pl.BlockSpec((1, LANES), lambda i, s: (0, 0))],
        out_specs=[row, row, pl.BlockSpec((tm, LANES), lambda i, s: (i, 0))])
    return pl.pallas_call(
        functools.partial(_out_router_kernel, nci=nci), grid_spec=gs,
        out_shape=[jax.ShapeDtypeStruct((nt, d), F32), jax.ShapeDtypeStruct((nt, d), F32),
                   jax.ShapeDtypeStruct((nt, LANES), F32)],
        compiler_params=_cparams(("arbitrary",)),
        name="out_proj_router",
    )(seg_ids, *mix_ctx, *mix_lat, w_out_bf, x, mods, g.reshape(1, d), mods, mods, wr, br)


def moe_plan(route, tm):
    nt = route.shape[0]
    npair = nt * TOP_K
    ns = npair + N_EXPERTS * tm
    n_tiles = ns // tm
    e = route[:, :TOP_K].astype(jnp.int32).reshape(npair)
    onehot = (e[:, None] == jnp.arange(N_EXPERTS, dtype=jnp.int32)[None, :]).astype(jnp.int32)
    csum = jnp.cumsum(onehot, axis=0)
    counts = csum[-1]
    rank = jnp.sum((csum - onehot) * onehot, axis=1)
    padded = ((counts + tm - 1) // tm) * tm
    ends = jnp.cumsum(padded)
    starts = ends - padded
    slot = (jnp.sum(onehot * starts[None, :], axis=1) + rank).astype(jnp.int32)
    tile_start = jnp.arange(n_tiles, dtype=jnp.int32) * tm
    tile_e = jnp.sum((tile_start[:, None] >= ends[None, :]).astype(jnp.int32), axis=1)
    tile_valid = (tile_start < ends[-1]).astype(jnp.int32)
    last_e = jnp.max(jnp.where(counts > 0, jnp.arange(N_EXPERTS, dtype=jnp.int32), 0))
    tile_e = jnp.where(tile_valid > 0, tile_e, last_e).astype(jnp.int32)
    prev = jnp.concatenate([jnp.full((1,), -1, jnp.int32), tile_e[:-1]])
    tile_new = (tile_e != prev).astype(jnp.int32)
    meta = jnp.stack([tile_e, tile_valid, tile_new], axis=0)
    return meta, slot


ROW_UNROLL = 8


def _dispatch_kernel(slot_ref, h_ref, xs_in, xs_hbm, sem):
    del xs_in
    te = h_ref.shape[0]

    def row_copy(r, k):
        return pltpu.make_async_copy(h_ref.at[pl.ds(r, 1)], xs_hbm.at[pl.ds(slot_ref[0, TOP_K * r + k], 1)], sem)

    def start(r, c):
        for k in range(TOP_K):
            row_copy(r, k).start(priority=k)
        return c

    lax.fori_loop(0, te, start, 0, unroll=ROW_UNROLL)

    def wait(r, c):
        for k in range(TOP_K):
            row_copy(r, k).wait()
        return c

    lax.fori_loop(0, te, wait, 0, unroll=ROW_UNROLL)


def moe_dispatch(h, slot, xs_prev, te):
    nt, d = h.shape
    n = nt // te
    ns = xs_prev.shape[0]
    return pl.pallas_call(
        _dispatch_kernel, grid=(n,),
        in_specs=[pl.BlockSpec((None, 1, TOP_K * te), lambda i: (i, 0, 0), memory_space=pltpu.SMEM),
                  pl.BlockSpec((te, d), lambda i: (i, 0)),
                  pl.BlockSpec(memory_space=pl.ANY)],
        out_specs=pl.BlockSpec(memory_space=pl.ANY),
        out_shape=jax.ShapeDtypeStruct((ns, d), F32),
        scratch_shapes=[pltpu.SemaphoreType.DMA],
        input_output_aliases={2: 0},
        compiler_params=_cparams(("arbitrary",)),
        name="moe_dispatch",
    )(slot.reshape(n, 1, TOP_K * te), h, xs_prev)


def _moe_kernel(meta_ref, x_ref, wg_ref, wu_ref, wd_ref, y_ref, wgb, wub, wdb):
    t = pl.program_id(0)

    @pl.when(meta_ref[1, t] > 0)
    def _():
        @pl.when(meta_ref[2, t] > 0)
        def _():
            wgb[...] = wg_ref[...].astype(BF16)
            wub[...] = wu_ref[...].astype(BF16)
            wdb[...] = wd_ref[...].astype(BF16)

        x = x_ref[...].astype(BF16)
        a = _dot(x, wgb[...])
        u = _dot(x, wub[...])
        hid = a * (1.0 / (1.0 + jnp.exp(-a))) * u
        y_ref[...] = _dot(hid.astype(BF16), wdb[...])

    @pl.when(meta_ref[1, t] == 0)
    def _():
        y_ref[...] = jnp.zeros_like(y_ref)


def moe_experts(xs, meta, w_gate, w_up, w_down, layer, tm):
    ns, d = xs.shape
    de = w_gate.shape[3]
    gs = pltpu.PrefetchScalarGridSpec(
        num_scalar_prefetch=1, grid=(ns // tm,),
        in_specs=[pl.BlockSpec((tm, d), lambda t, m: (t, 0)),
                  pl.BlockSpec((None, None, d, de), lambda t, m: (layer, m[0, t], 0, 0)),
                  pl.BlockSpec((None, None, d, de), lambda t, m: (layer, m[0, t], 0, 0)),
                  pl.BlockSpec((None, None, de, d), lambda t, m: (layer, m[0, t], 0, 0))],
        out_specs=pl.BlockSpec((tm, d), lambda t, m: (t, 0)),
        scratch_shapes=[pltpu.VMEM((d, de), BF16), pltpu.VMEM((d, de), BF16), pltpu.VMEM((de, d), BF16)])
    return pl.pallas_call(
        _moe_kernel, grid_spec=gs,
        out_shape=jax.ShapeDtypeStruct((ns, d), F32),
        compiler_params=_cparams(("arbitrary",)),
        name="moe_experts",
    )(meta, xs, w_gate, w_up, w_down)


def _moe_combine_kernel(s_ref, slot_ref, nslot_ref, x_ref, rt_ref, g2_ref, g_ref, sc_ref, sh_ref, ys_hbm,
                        x2_ref, h_ref, ybuf, sem, *, modulate):
    i = pl.program_id(0)
    n = pl.num_programs(0)
    te = x_ref.shape[0]

    def row_copy(idx_ref, buf, r, k):
        return pltpu.make_async_copy(ys_hbm.at[pl.ds(idx_ref[0, TOP_K * r + k], 1)],
                                     ybuf.at[buf, k, pl.ds(r, 1)], sem.at[buf])

    def start_tile(idx_ref, buf):
        def body(r, c):
            for k in range(TOP_K):
                row_copy(idx_ref, buf, r, k).start(priority=k)
            return c
        lax.fori_loop(0, te, body, 0, unroll=ROW_UNROLL)

    cur = i % 2

    @pl.when(i == 0)
    def _():
        start_tile(slot_ref, 0)

    @pl.when(i + 1 < n)
    def _():
        start_tile(nslot_ref, 1 - cur)

    def wait_body(r, c):
        for k in range(TOP_K):
            row_copy(slot_ref, cur, r, k).wait()
        return c

    lax.fori_loop(0, te, wait_body, 0, unroll=ROW_UNROLL)
    w1 = rt_ref[:, TOP_K:TOP_K + 1]
    w2 = rt_ref[:, TOP_K + 1:TOP_K + 2]
    x2 = x_ref[...] + g2_ref[...] * (w1 * ybuf[cur, 0] + w2 * ybuf[cur, 1])
    x2_ref[...] = x2
    h = _rms(x2) * g_ref[...]
    if modulate:
        h = h * (1.0 + sc_ref[...]) + sh_ref[...]
    h_ref[...] = h.astype(h_ref.dtype)


def moe_combine(x1, ys, slot, route, g, mods_prev, mods_next, seg_ids, te, modulate, h_dtype):
    nt, d = x1.shape
    n = nt // te
    slot3 = slot.reshape(n, 1, TOP_K * te)
    gs = pltpu.PrefetchScalarGridSpec(
        num_scalar_prefetch=1, grid=(n,),
        in_specs=[pl.BlockSpec((None, 1, TOP_K * te), lambda i, s: (i, 0, 0), memory_space=pltpu.SMEM),
                  pl.BlockSpec((None, 1, TOP_K * te), lambda i, s: (jnp.minimum(i + 1, n - 1), 0, 0),
                               memory_space=pltpu.SMEM),
                  pl.BlockSpec((te, d), lambda i, s: (i, 0)),
                  pl.BlockSpec((te, LANES), lambda i, s: (i, 0)),
                  pl.BlockSpec((None, 1, d), lambda i, s: (s[i], 0, 5)),
                  pl.BlockSpec((1, d), lambda i, s: (0, 0)),
                  pl.BlockSpec((None, 1, d), lambda i, s: (s[i], 0, 1)),
                  pl.BlockSpec((None, 1, d), lambda i, s: (s[i], 0, 0)),
                  pl.BlockSpec(memory_space=pl.ANY)],
        out_specs=[pl.BlockSpec((te, d), lambda i, s: (i, 0)),
                   pl.BlockSpec((te, d), lambda i, s: (i, 0))],
        scratch_shapes=[pltpu.VMEM((2, TOP_K, te, d), F32), pltpu.SemaphoreType.DMA((2,))])
    return pl.pallas_call(
        functools.partial(_moe_combine_kernel, modulate=modulate), grid_spec=gs,
        out_shape=[jax.ShapeDtypeStruct((nt, d), F32), jax.ShapeDtypeStruct((nt, d), h_dtype)],
        compiler_params=_cparams(("arbitrary",)),
        name="moe_combine",
    )(seg_ids, slot3, slot3, x1, route, mods_prev, g.reshape(1, d), mods_next, mods_next, ys)


def kernel(x_prompt, x_sample, c, cache_na_k, cache_na_v, cache_gqa_k, cache_gqa_v, c_ctx, norm1_g, norm2_g, final_norm_g, ada_w, ada_b, w_in, w_out, na_rpb, q_norm_g, k_norm_g, hy_conv_w, hy_conv_b, hy_w1, hy_b1, hy_w2, hy_b2, hy_w3, hy_b3, hy_freq, hy_w_out, hy_bias, router_group_w, router_group_b, router_expert_w, router_expert_b, moe_w_gate, moe_w_up, moe_w_down):
    bc, lc, d = x_prompt.shape
    bl, ll, _ = x_sample.shape
    depth = w_in.shape[0]
    past = cache_na_k.shape[2]
    dh = HEAD_DIM
    na_w, gq_w = d // 4, d // 2
    hy_w = d - na_w - gq_w
    na_heads = na_w // dh
    nc = bc * lc
    nt = nc + bl * ll
    tm = 512
    moe_tm = 256
    assert nc % tm == 0 and ll % tm == 0 and past == lc

    te = 256

    def seg_of_tiles(t):
        return jnp.asarray(np.concatenate(
            [np.zeros(nc // t), 1 + np.repeat(np.arange(bl), ll // t)]).astype(np.int32))

    seg_e = seg_of_tiles(te)
    seg_m = seg_of_tiles(tm)
    nseg = 1 + bl
    cond = jnp.zeros((8, d), F32).at[0].set(c_ctx).at[1:1 + bl].set(c)
    mods_all = modulation_all(cond, ada_w, ada_b)

    ropes = rope_tables(ll)
    tabs_c = dft_tables(lc)
    tabs_l = dft_tables(ll)
    ck_na = cache_na_k.reshape(bl, depth, past, na_w)
    cv_na = cache_na_v.reshape(bl, depth, past, na_w)
    ck_gq = cache_gqa_k.reshape(bl, depth, past, GQA_KV_HEADS * dh)
    cv_gq = cache_gqa_v.reshape(bl, depth, past, GQA_KV_HEADS * dh)

    x = jnp.concatenate([x_prompt.reshape(nc, d), x_sample.reshape(bl * ll, d)], axis=0)
    mods = mods_all[0, :nseg].reshape(nseg, 1, 6 * d)
    h = norm_mod(x, norm1_g[0], mods, 1, 0, seg_e, te, BF16)
    st_na_k, st_na_v, st_gq_k, st_gq_v = [], [], [], []
    hy_col0 = 3 * na_w + gq_w + 2 * GQA_KV_HEADS * dh
    moe_x = jnp.zeros((nt * TOP_K + N_EXPERTS * moe_tm, d), F32)
    for l in range(depth):
        z = in_proj(h, w_in, l)
        zc = z[:nc]
        st_na_k.append(zc[:, na_w:2 * na_w].reshape(bc, lc, na_heads, dh))
        st_na_v.append(zc[:, 2 * na_w:3 * na_w].reshape(bc, lc, na_heads, dh))
        st_gq_v.append(zc[:, 3 * na_w + gq_w + GQA_KV_HEADS * dh:hy_col0].reshape(bc, lc, GQA_KV_HEADS, dh))

        ctx_a, ctx_b, bk = ctx_attn(z, q_norm_g[l], k_norm_g[l], bc, lc, na_w, gq_w)
        st_gq_k.append(bk.reshape(bc, lc, GQA_KV_HEADS, dh))

        bias = na_bias(na_rpb[l])
        lat_a = lat_na(z, ck_na, cv_na, l, bias, nc, bl, ll, na_w)
        q, k, v = lat_prep(z, ropes, q_norm_g[l], k_norm_g[l], nc, bl, ll, gq_w)
        lat_b = lat_gqa(q, k, v, ck_gq, cv_gq, l)

        mix_c = []
        for (row0, nb, L, tabs) in ((0, bc, lc, tabs_c), (nc, bl, ll, tabs_l)):
            ctab, s1tab, s2tab = tabs
            x0, zz, zb = hy_pre(z, hy_conv_w[l], hy_conv_b[l], row0, nb, L, hy_col0, hy_w)
            filt = hy_filter(L, hy_w1[l], hy_b1[l], hy_w2[l], hy_b2[l], hy_w3[l], hy_b3[l],
                             hy_freq[l], hy_w_out[l])
            kp, kq = dft_fwd(ctab, s1tab, filt)
            yr, yi = dft_mul(ctab, s1tab, zb, kp, kq)
            mix_c.append(dft_inv(ctab, s2tab, yr, yi, x0, zz, hy_bias[l]))

        x1, h2, route = out_proj_router((ctx_a, ctx_b, mix_c[0]), (lat_a, lat_b, mix_c[1]),
                                        w_out[l].astype(BF16), x, norm2_g[l], mods, seg_e, te,
                                        router_group_w[l], router_group_b[l],
                                        router_expert_w[l], router_expert_b[l])
        meta, slot = moe_plan(route, moe_tm)
        moe_x = moe_dispatch(h2, slot, moe_x, tm)
        ys = moe_experts(moe_x, meta, moe_w_gate, moe_w_up, moe_w_down, l, moe_tm)
        last = l == depth - 1
        mods_next = mods if last else mods_all[l + 1, :nseg].reshape(nseg, 1, 6 * d)
        g_next = final_norm_g if last else norm1_g[l + 1]
        x, h = moe_combine(x1, ys, slot, route, g_next, mods, mods_next, seg_m, tm, not last,
                           F32 if last else BF16)
        mods = mods_next

    y_prompt = h[:nc].reshape(bc, lc, d)
    y_sample = h[nc:].reshape(bl, ll, d)
    return (y_prompt, y_sample, jnp.stack(st_na_k, axis=1), jnp.stack(st_na_v, axis=1),
            jnp.stack(st_gq_k, axis=1), jnp.stack(st_gq_v, axis=1))
```
